```python
import math, functools
import jax, jax.numpy as jnp
from jax import lax
import numpy as np

D_MODEL = 1024
BATCH = 2
SEQ = 8192
DEPTH = 4
DEC_BATCH = 32
DEC_SEQ = 1
PAST_LEN = 8192
PAGE_SIZE = 128

N_BRANCH = 4
BR_WIDTH = D_MODEL // 4
RET_HEADS = 4
RET_DK = BR_WIDTH // RET_HEADS
RET_DV = BR_WIDTH // RET_HEADS
RET_CHUNK = 128
ROPE_BASE = 10000.0
SSM_GROUP_CH = 16
SSM_GROUPS = BR_WIDTH // SSM_GROUP_CH
SSM_STATE = 64
GLA_HEADS = 4
GLA_DV = BR_WIDTH // GLA_HEADS
GLA_DK = GLA_DV // 2
GLA_RANK = 16
GLA_TAU = 16.0
GLA_CHUNK = 32
FOX_HEADS = 4
FOX_HD = BR_WIDTH // FOX_HEADS
FOX_QBLOCK = 128
D_FF = 2816
CONV_W = 3
EPS = 1e-6

IN_SIZES = (RET_HEADS * RET_DK, RET_HEADS * RET_DK, RET_HEADS * RET_DV, RET_HEADS * RET_DV,
            SSM_GROUPS * SSM_GROUP_CH,
            GLA_HEADS * GLA_DK, GLA_HEADS * GLA_DK, GLA_HEADS * GLA_DV, GLA_HEADS * GLA_DV, GLA_RANK,
            FOX_HEADS * FOX_HD, FOX_HEADS * FOX_HD, FOX_HEADS * FOX_HD, FOX_HEADS)
IN_SPLITS = tuple(sum(IN_SIZES[:i + 1]) for i in range(len(IN_SIZES) - 1))
D_IN = sum(IN_SIZES)

kernel_name = 'hybrid_retention_s5_gla_fox_convffn_step'


def _rmsnorm(x, g):
    xf = x.astype(jnp.float32)
    y = xf * lax.rsqrt(jnp.mean(xf * xf, axis=-1, keepdims=True) + EPS)
    return (y * g.astype(jnp.float32)).astype(x.dtype)


def _head_norm(o, g, center):
    if center:
        o = o - jnp.mean(o, axis=-1, keepdims=True)
    o = o * lax.rsqrt(jnp.mean(o * o, axis=-1, keepdims=True) + EPS)
    B, L, H, dv = o.shape
    return o.reshape(B, L, H * dv) * g.astype(jnp.float32)


def _rope(x, pos):
    d = x.shape[-1]
    half = d // 2
    inv = ROPE_BASE ** (-jnp.arange(half, dtype=jnp.float32) * 2.0 / d)
    ang = pos.astype(jnp.float32)[:, None] * inv[None, :]
    cos = jnp.cos(ang)[None, :, None, :]
    sin = jnp.sin(ang)[None, :, None, :]
    x = x.astype(jnp.float32)
    x1, x2 = x[..., :half], x[..., half:]
    return jnp.concatenate([x1 * cos - x2 * sin, x1 * sin + x2 * cos], axis=-1)


def _block(L, c):
    return c if L % c == 0 else L


def _chunked_linear_attention(q, k, v, log_a, s0, chunk):
    B, L, H, _ = q.shape
    dv = v.shape[-1]
    da = log_a.shape[-1]
    n = L // chunk

    def to_chunks(t):
        return t.astype(jnp.float32).reshape(B, n, chunk, H, t.shape[-1]).transpose(1, 0, 3, 2, 4)

    qc, kc, vc, ac = to_chunks(q), to_chunks(k), to_chunks(v), to_chunks(log_a)
    causal = jnp.tril(jnp.ones((chunk, chunk), dtype=bool))

    def step(S, inp):
        qi, ki, vi, ai = inp
        b = jnp.cumsum(ai, axis=2)
        b_last = b[:, :, -1:, :]
        diff = b[:, :, :, None, :] - b[:, :, None, :, :]
        decay = jnp.exp(jnp.where(causal[:, :, None], diff, -jnp.inf))
        if da == 1:
            scores = jnp.einsum('bhtd,bhsd->bhts', qi, ki) * decay[..., 0]
        else:
            scores = jnp.einsum('bhtd,bhsd,bhtsd->bhts', qi, ki, decay)
        o = (jnp.einsum('bhts,bhse->bhte', scores, vi)
             + jnp.einsum('bhtd,bhde->bhte', qi * jnp.exp(b), S))
        S_new = (jnp.exp(b_last)[:, :, 0, :, None] * S
                 + jnp.einsum('bhsd,bhse->bhde', ki * jnp.exp(b_last - b), vi))
        return S_new, o

    S, o = lax.scan(step, s0.astype(jnp.float32), (qc, kc, vc, ac))
    o = o.transpose(1, 0, 3, 2, 4).reshape(B, L, H, dv)
    return o, S


def _s5_branch(u, re0, im0, a_re, a_im, log_dt, b_re, b_im, c_re, c_im, d_skip, w_glu, b_glu):
    B, L, _ = u.shape
    uf = u.astype(jnp.float32)
    ug = uf.reshape(B, L, SSM_GROUPS, SSM_GROUP_CH)
    a_re = a_re.astype(jnp.float32)
    a_im = a_im.astype(jnp.float32)
    dt = jnp.exp(log_dt.astype(jnp.float32))[:, None]
    mag = jnp.exp(a_re * dt)
    ab_re = mag * jnp.cos(a_im * dt)
    ab_im = mag * jnp.sin(a_im * dt)
    den = a_re * a_re + a_im * a_im
    f_re = ((ab_re - 1.0) * a_re + ab_im * a_im) / den
    f_im = (ab_im * a_re - (ab_re - 1.0) * a_im) / den
    b_re = b_re.astype(jnp.float32)
    b_im = b_im.astype(jnp.float32)
    bb_re = f_re[..., None] * b_re - f_im[..., None] * b_im
    bb_im = f_re[..., None] * b_im + f_im[..., None] * b_re
    bu_re = jnp.einsum('blgc,gpc->blgp', ug, bb_re)
    bu_im = jnp.einsum('blgc,gpc->blgp', ug, bb_im)
    re0 = re0.astype(jnp.float32)
    im0 = im0.astype(jnp.float32)
    bu_re = bu_re.at[:, 0].add(ab_re * re0 - ab_im * im0)
    bu_im = bu_im.at[:, 0].add(ab_re * im0 + ab_im * re0)
    ar = jnp.broadcast_to(ab_re, bu_re.shape)
    ai = jnp.broadcast_to(ab_im, bu_im.shape)

    def combine(e1, e2):
        a1r, a1i, b1r, b1i = e1
        a2r, a2i, b2r, b2i = e2
        return (a2r * a1r - a2i * a1i, a2r * a1i + a2i * a1r,
                a2r * b1r - a2i * b1i + b2r, a2r * b1i + a2i * b1r + b2i)

    _, _, xr, xi = lax.associative_scan(combine, (ar, ai, bu_re, bu_im), axis=1)
    y = (jnp.einsum('blgp,gcp->blgc', xr, c_re.astype(jnp.float32))
         - jnp.einsum('blgp,gcp->blgc', xi, c_im.astype(jnp.float32)))
    y = y.reshape(B, L, BR_WIDTH) + d_skip.astype(jnp.float32) * uf
    yg = jax.nn.gelu(y)
    y = yg * jax.nn.sigmoid(yg @ w_glu.astype(jnp.float32) + b_glu.astype(jnp.float32))
    return y, xr[:, -1], xi[:, -1]


def _fox_attend(q, c_q, qpos, k, v, c_k, kpos):
    s = jnp.einsum('bqhd,bkhd->bhqk', q.astype(jnp.float32), k.astype(jnp.float32)) * (FOX_HD ** -0.5)
    s = s + jnp.transpose(c_q, (0, 2, 1))[:, :, :, None] - jnp.transpose(c_k, (0, 2, 1))[:, :, None, :]
    s = jnp.where(kpos[None, :] <= qpos[:, None], s, -jnp.inf)
    p = jax.nn.softmax(s, axis=-1)
    return jnp.einsum('bhqk,bkhd->bqhd', p, v.astype(jnp.float32))


def _fox_prompt(q, k, v, logf, pos):
    B, L, H, d = q.shape
    c = jnp.cumsum(logf.astype(jnp.float32), axis=1)
    qb = _block(L, FOX_QBLOCK)
    nb = L // qb
    blocks = (q.reshape(B, nb, qb, H, d).transpose(1, 0, 2, 3, 4),
              c.reshape(B, nb, qb, H).transpose(1, 0, 2, 3),
              pos.reshape(nb, qb))
    out = lax.map(lambda blk: _fox_attend(blk[0], blk[1], blk[2], k, v, c, pos), blocks)
    return out.transpose(1, 0, 2, 3, 4).reshape(B, L, H, d)


def _fox_sample(q, k, v, logf, pos, k_past, v_past, lf_past):
    P = k_past.shape[1]
    L = q.shape[1]
    k_all = jnp.concatenate([k_past, k], axis=1)
    v_all = jnp.concatenate([v_past, v], axis=1)
    c_all = jnp.cumsum(jnp.concatenate([lf_past.astype(jnp.float32), logf.astype(jnp.float32)], axis=1), axis=1)
    kpos = jnp.arange(P + L)
    return _fox_attend(q, c_all[:, P:], pos, k_all, v_all, c_all, kpos)


def _layer(x, pos, p, ret_s0, ssm_re0, ssm_im0, gla_s0, conv0, fox_fn):
    B, L, _ = x.shape
    h = _rmsnorm(x, p['norm1_g'])
    proj = h @ p['w_in']
    (rq, rk, rv, rg, su, gq, gk, gv, gg, ga, fq, fk, fv, ff) = jnp.split(proj, IN_SPLITS, axis=-1)

    q = _rope(rq.reshape(B, L, RET_HEADS, RET_DK), pos)
    k = _rope(rk.reshape(B, L, RET_HEADS, RET_DK), pos) * (RET_DK ** -0.5)
    v = rv.reshape(B, L, RET_HEADS, RET_DV)
    log_gamma = jnp.log(1.0 - 2.0 ** (-5.0 - jnp.arange(RET_HEADS, dtype=jnp.float32)))
    log_a = jnp.broadcast_to(log_gamma[None, None, :, None], (B, L, RET_HEADS, 1))
    o, ret_S = _chunked_linear_attention(q, k, v, log_a, ret_s0, _block(L, RET_CHUNK))
    o_ret = _head_norm(o, p['ret_norm_g'], True) * jax.nn.silu(rg.astype(jnp.float32))

    o_ssm, ssm_re, ssm_im = _s5_branch(su, ssm_re0, ssm_im0, p['ssm_a_re'], p['ssm_a_im'], p['ssm_log_dt'],
                                       p['ssm_b_re'], p['ssm_b_im'], p['ssm_c_re'], p['ssm_c_im'],
                                       p['ssm_d'], p['ssm_w_glu'], p['ssm_b_glu'])

    q = gq.reshape(B, L, GLA_HEADS, GLA_DK)
    k = gk.reshape(B, L, GLA_HEADS, GLA_DK) * (GLA_DK ** -0.5)
    v = gv.reshape(B, L, GLA_HEADS, GLA_DV)
    log_alpha = jax.nn.log_sigmoid((ga @ p['gla_w_a2'] + p['gla_b_a']).astype(jnp.float32)) / GLA_TAU
    o, gla_S = _chunked_linear_attention(q, k, v, log_alpha.reshape(B, L, GLA_HEADS, GLA_DK), gla_s0,
                                         _block(L, GLA_CHUNK))
    o_gla = _head_norm(o, p['gla_norm_g'], False) * jax.nn.silu(gg.astype(jnp.float32))

    fq = fq.reshape(B, L, FOX_HEADS, FOX_HD)
    fk = fk.reshape(B, L, FOX_HEADS, FOX_HD)
    fv = fv.reshape(B, L, FOX_HEADS, FOX_HD)
    logf = jax.nn.log_sigmoid(ff.astype(jnp.float32) + p['fox_b_f'].astype(jnp.float32))
    o_fox = fox_fn(fq, fk, fv, logf, pos).reshape(B, L, BR_WIDTH)

    o_stack = jnp.stack([o_ret, o_ssm, o_gla, o_fox], axis=2)
    proj_br = jnp.einsum('blnc,ncd->blnd', o_stack, p['w_branch'])
    gates = jax.nn.sigmoid((h @ p['w_mgate'] + p['b_mgate']).astype(jnp.float32)).reshape(B, L, N_BRANCH, D_MODEL)
    merged = jnp.sum(gates * proj_br, axis=2)
    x = x + (merged @ p['w_out']).astype(x.dtype)

    h2 = _rmsnorm(x, p['norm2_g'])
    up = h2 @ p['ffn_w_up']
    a, g = jnp.split(up, 2, axis=-1)
    a_ext = jnp.concatenate([conv0.astype(a.dtype), a], axis=1)
    conv = p['ffn_conv_b'] + sum(p['ffn_conv_w'][j] * a_ext[:, j:j + L] for j in range(CONV_W))
    new_conv = a_ext[:, L:]
    y = (jax.nn.gelu(conv) * g) @ p['ffn_w_down']
    x = x + y.astype(x.dtype)
    return x, (ret_S, ssm_re, ssm_im, gla_S, new_conv, fk, fv, logf)


def setup_inputs(seed: int = 0) -> dict:
    key = jax.random.key(seed)
    ks = iter(jax.random.split(key, 48))
    f32 = jnp.float32

    def nrm(shape, scale):
        return jax.random.normal(next(ks), shape, f32) * scale

    n_pages = PAST_LEN // PAGE_SIZE
    n_used = DEC_BATCH * n_pages
    n_pool = n_used + max(1, n_used // 4)
    page_table = jax.random.permutation(next(ks), n_pool)[:n_used].reshape(DEC_BATCH, n_pages).astype(jnp.int32)

    x_prompt = nrm((BATCH, SEQ, D_MODEL), 1.0)
    x_sample = nrm((DEC_BATCH, DEC_SEQ, D_MODEL), 1.0)
    cache_k = nrm((DEPTH, n_pool, PAGE_SIZE, FOX_HEADS, FOX_HD), 1.0)
    cache_v = nrm((DEPTH, n_pool, PAGE_SIZE, FOX_HEADS, FOX_HD), 1.0)
    cache_logf = jax.nn.log_sigmoid(3.0 + nrm((DEPTH, n_pool, PAGE_SIZE, FOX_HEADS), 1.0))
    state_ret = nrm((DEPTH, DEC_BATCH, RET_HEADS, RET_DK, RET_DV), 0.5)
    state_ssm_re = nrm((DEPTH, DEC_BATCH, SSM_GROUPS, SSM_STATE), 0.5)
    state_ssm_im = nrm((DEPTH, DEC_BATCH, SSM_GROUPS, SSM_STATE), 0.5)
    state_gla = nrm((DEPTH, DEC_BATCH, GLA_HEADS, GLA_DK, GLA_DV), 0.5)
    state_conv = nrm((DEPTH, DEC_BATCH, CONV_W - 1, D_FF), 0.5)

    norm1_g = 1.0 + nrm((DEPTH, D_MODEL), 0.02)
    w_in = nrm((DEPTH, D_MODEL, D_IN), D_MODEL ** -0.5)
    w_mgate = nrm((DEPTH, D_MODEL, N_BRANCH * D_MODEL), D_MODEL ** -0.5)
    b_mgate = nrm((DEPTH, N_BRANCH * D_MODEL), 0.02)
    ret_norm_g = 1.0 + nrm((DEPTH, BR_WIDTH), 0.02)
    ssm_a_re = -0.5 + nrm((DEPTH, SSM_GROUPS, SSM_STATE), 0.01)
    ssm_a_im = jnp.pi * jnp.arange(SSM_STATE, dtype=f32)[None, None, :] + nrm((DEPTH, SSM_GROUPS, SSM_STATE), 0.01)
    ssm_log_dt = jax.random.uniform(next(ks), (DEPTH, SSM_GROUPS), f32, math.log(1e-3), math.log(1e-1))
    ssm_b_re = nrm((DEPTH, SSM_GROUPS, SSM_STATE, SSM_GROUP_CH), (2.0 * SSM_GROUP_CH) ** -0.5)
    ssm_b_im = nrm((DEPTH, SSM_GROUPS, SSM_STATE, SSM_GROUP_CH), (2.0 * SSM_GROUP_CH) ** -0.5)
    ssm_c_re = nrm((DEPTH, SSM_GROUPS, SSM_GROUP_CH, SSM_STATE), (2.0 * SSM_STATE) ** -0.5)
    ssm_c_im = nrm((DEPTH, SSM_GROUPS, SSM_GROUP_CH, SSM_STATE), (2.0 * SSM_STATE) ** -0.5)
    ssm_d = nrm((DEPTH, BR_WIDTH), 0.5)
    ssm_w_glu = nrm((DEPTH, BR_WIDTH, BR_WIDTH), BR_WIDTH ** -0.5)
    ssm_b_glu = nrm((DEPTH, BR_WIDTH), 0.02)
    gla_w_a2 = nrm((DEPTH, GLA_RANK, GLA_HEADS * GLA_DK), GLA_RANK ** -0.5)
    gla_b_a = nrm((DEPTH, GLA_HEADS * GLA_DK), 0.02)
    gla_norm_g = 1.0 + nrm((DEPTH, BR_WIDTH), 0.02)
    fox_b_f = jax.random.uniform(next(ks), (DEPTH, FOX_HEADS), f32, 1.0, 4.0)
    w_branch = nrm((DEPTH, N_BRANCH, BR_WIDTH, D_MODEL), BR_WIDTH ** -0.5)
    w_out = nrm((DEPTH, D_MODEL, D_MODEL), D_MODEL ** -0.5)
    norm2_g = 1.0 + nrm((DEPTH, D_MODEL), 0.02)
    ffn_w_up = nrm((DEPTH, D_MODEL, 2 * D_FF), D_MODEL ** -0.5)
    ffn_conv_w = nrm((DEPTH, CONV_W, D_FF), CONV_W ** -0.5)
    ffn_conv_b = nrm((DEPTH, D_FF), 0.02)
    ffn_w_down = nrm((DEPTH, D_FF, D_MODEL), D_FF ** -0.5)
    final_norm_g = 1.0 + nrm((D_MODEL,), 0.02)
    return {'x_prompt': x_prompt, 'x_sample': x_sample, 'cache_k': cache_k, 'cache_v': cache_v,
            'cache_logf': cache_logf, 'state_ret': state_ret, 'state_ssm_re': state_ssm_re,
            'state_ssm_im': state_ssm_im, 'state_gla': state_gla, 'state_conv': state_conv,
            'page_table': page_table, 'norm1_g': norm1_g, 'w_in': w_in, 'w_mgate': w_mgate,
            'b_mgate': b_mgate, 'ret_norm_g': ret_norm_g, 'ssm_a_re': ssm_a_re, 'ssm_a_im': ssm_a_im,
            'ssm_log_dt': ssm_log_dt, 'ssm_b_re': ssm_b_re, 'ssm_b_im': ssm_b_im, 'ssm_c_re': ssm_c_re,
            'ssm_c_im': ssm_c_im, 'ssm_d': ssm_d, 'ssm_w_glu': ssm_w_glu, 'ssm_b_glu': ssm_b_glu,
            'gla_w_a2': gla_w_a2, 'gla_b_a': gla_b_a, 'gla_norm_g': gla_norm_g, 'fox_b_f': fox_b_f,
            'w_branch': w_branch, 'w_out': w_out, 'norm2_g': norm2_g, 'ffn_w_up': ffn_w_up,
            'ffn_conv_w': ffn_conv_w, 'ffn_conv_b': ffn_conv_b, 'ffn_w_down': ffn_w_down,
            'final_norm_g': final_norm_g}


def reference(x_prompt, x_sample, cache_k, cache_v, cache_logf, state_ret, state_ssm_re, state_ssm_im,
              state_gla, state_conv, page_table, norm1_g, w_in, w_mgate, b_mgate, ret_norm_g, ssm_a_re,
              ssm_a_im, ssm_log_dt, ssm_b_re, ssm_b_im, ssm_c_re, ssm_c_im, ssm_d, ssm_w_glu, ssm_b_glu,
              gla_w_a2, gla_b_a, gla_norm_g, fox_b_f, w_branch, w_out, norm2_g, ffn_w_up, ffn_conv_w,
              ffn_conv_b, ffn_w_down, final_norm_g):
    B, L = x_prompt.shape[0], x_prompt.shape[1]
    DB, LS = x_sample.shape[0], x_sample.shape[1]
    past = page_table.shape[1] * PAGE_SIZE
    pos_p = jnp.arange(L)
    pos_s = past + jnp.arange(LS)

    xp, xs = x_prompt, x_sample
    outs_p = []
    outs_s = []
    for l in range(DEPTH):
        p = dict(norm1_g=norm1_g[l], w_in=w_in[l], w_mgate=w_mgate[l], b_mgate=b_mgate[l],
                 ret_norm_g=ret_norm_g[l], ssm_a_re=ssm_a_re[l], ssm_a_im=ssm_a_im[l],
                 ssm_log_dt=ssm_log_dt[l], ssm_b_re=ssm_b_re[l], ssm_b_im=ssm_b_im[l],
                 ssm_c_re=ssm_c_re[l], ssm_c_im=ssm_c_im[l], ssm_d=ssm_d[l], ssm_w_glu=ssm_w_glu[l],
                 ssm_b_glu=ssm_b_glu[l], gla_w_a2=gla_w_a2[l], gla_b_a=gla_b_a[l], gla_norm_g=gla_norm_g[l],
                 fox_b_f=fox_b_f[l], w_branch=w_branch[l], w_out=w_out[l], norm2_g=norm2_g[l],
                 ffn_w_up=ffn_w_up[l], ffn_conv_w=ffn_conv_w[l], ffn_conv_b=ffn_conv_b[l],
                 ffn_w_down=ffn_w_down[l])
        xp, st_p = _layer(xp, pos_p, p,
                          jnp.zeros((B, RET_HEADS, RET_DK, RET_DV), jnp.float32),
                          jnp.zeros((B, SSM_GROUPS, SSM_STATE), jnp.float32),
                          jnp.zeros((B, SSM_GROUPS, SSM_STATE), jnp.float32),
                          jnp.zeros((B, GLA_HEADS, GLA_DK, GLA_DV), jnp.float32),
                          jnp.zeros((B, CONV_W - 1, D_FF), xp.dtype),
                          _fox_prompt)
        k_past = cache_k[l, page_table].reshape(DB, past, FOX_HEADS, FOX_HD)
        v_past = cache_v[l, page_table].reshape(DB, past, FOX_HEADS, FOX_HD)
        lf_past = cache_logf[l, page_table].reshape(DB, past, FOX_HEADS)
        fox_fn = functools.partial(_fox_sample, k_past=k_past, v_past=v_past, lf_past=lf_past)
        xs, st_s = _layer(xs, pos_s, p, state_ret[l], state_ssm_re[l], state_ssm_im[l], state_gla[l],
                          state_conv[l], fox_fn)
        outs_p.append(st_p)
        outs_s.append(st_s)

    y_prompt = _rmsnorm(xp, final_norm_g)
    y_sample = _rmsnorm(xs, final_norm_g)

    def stk(outs, i):
        return jnp.stack([o[i] for o in outs], axis=0)

    return (y_prompt, y_sample,
            stk(outs_p, 5), stk(outs_p, 6), stk(outs_p, 7),
            stk(outs_s, 5), stk(outs_s, 6), stk(outs_s, 7),
            stk(outs_p, 0), stk(outs_s, 0),
            stk(outs_p, 1), stk(outs_p, 2), stk(outs_s, 1), stk(outs_s, 2),
            stk(outs_p, 3), stk(outs_s, 3),
            stk(outs_p, 4), stk(outs_s, 4))
```

```python
import functools
import math

import jax
import jax.numpy as jnp
import numpy as np
from jax import lax
from jax.experimental import pallas as pl
from jax.experimental.pallas import tpu as pltpu

F32 = jnp.float32
BF16 = jnp.bfloat16

LANES = 128
SUBLANES = 8
VMEM_BYTES_V7X = 64 * 1024 * 1024
VMEM_LIMIT = VMEM_BYTES_V7X - 8 * 1024 * 1024

D_MODEL = 1024
BR = 256
N_BRANCH = 4
HEADS = 4
RET_DK = 64
RET_CHUNK = 128
SSM_G = 16
SSM_GC = 16
SSM_P = 64
SSM_N = SSM_G * SSM_P
GLA_DK = 32
GLA_KW = HEADS * GLA_DK
GLA_RANK = 16
GLA_TAU = 16.0
GLA_CHUNK = 32
FOX_HD = 64
D_FF = 2816
FF_CHUNK = 1408
CONV_W = 3
EPS = 1e-6
ROPE_BASE = 10000.0
PAGE = 128
PAGES_PER_STEP = 8
EXP_CLAMP = 80.0

C_RQ, C_RK, C_RV, C_RG, C_SU = 0, 256, 512, 768, 1024
C_GQ, C_GK, C_GV, C_GG = 1280, 1408, 1536, 1792
C_FQ, C_FK, C_FV, C_SM = 2048, 2304, 2560, 2816
W_IN_COLS = 2944
G_V, G_G, G_Q, G_K, G_LA = 0, 256, 512, 640, 768
N_RQ, N_RK, N_RV, N_RG, N_SU = 0, 256, 512, 768, 1024
N_GQ, N_GK, N_GV, N_GG, N_GA = 1280, 1408, 1536, 1792, 2048
N_FQ, N_FK, N_FV, N_FF = 2064, 2320, 2576, 2832


def _cparams(sem):
    return pltpu.CompilerParams(dimension_semantics=sem, vmem_limit_bytes=VMEM_LIMIT)


def _dot(a, b):
    return jnp.dot(a, b, preferred_element_type=F32)


def _dot_nt(a, b):
    return lax.dot_general(a, b, (((1,), (1,)), ((), ())), preferred_element_type=F32)


def _dot_tn(a, b):
    return lax.dot_general(a, b, (((0,), (0,)), ((), ())), preferred_element_type=F32)


def _split3(x):
    hi = x.astype(BF16)
    r = x - hi.astype(F32)
    mid = r.astype(BF16)
    lo = (r - mid.astype(F32)).astype(BF16)
    return hi, mid, lo


def _dot3_left(m, x):
    hi, mid, lo = _split3(x)
    return _dot(m, hi) + _dot(m, mid) + _dot(m, lo)


def _dot3_right(x, m):
    hi, mid, lo = _split3(x)
    return _dot(hi, m) + _dot(mid, m) + _dot(lo, m)


def _dot2_right(x, m):
    hi = x.astype(BF16)
    lo = (x - hi.astype(F32)).astype(BF16)
    return _dot(hi, m) + _dot(lo, m)


def _sigmoid(x):
    return 1.0 / (1.0 + jnp.exp(-x))


def _log_sigmoid(x):
    return jnp.minimum(x, 0.0) - jnp.log(1.0 + jnp.exp(-jnp.abs(x)))


def _gelu_tanh(x):
    return 0.5 * x * (1.0 + jnp.tanh(math.sqrt(2.0 / math.pi) * (x + 0.044715 * (x * x * x))))


def _rms(x, g):
    return x * lax.rsqrt(jnp.mean(x * x, axis=-1, keepdims=True) + EPS) * g


def _iota(shape, dim):
    return lax.broadcasted_iota(jnp.int32, shape, dim)


def _head_avg(width, hd):
    r = _iota((width, width), 0) // hd
    c = _iota((width, width), 1) // hd
    return jnp.where(r == c, 1.0 / hd, 0.0).astype(BF16)


def _full(shape):
    nd = len(shape)
    return pl.BlockSpec(shape, lambda *a, _nd=nd: (0,) * _nd)


def _s5_prep_kernel(are_ref, aim_ref, ldt_ref, bre_ref, bim_ref, abre_ref, abim_ref, bbre_ref, bbim_ref):
    a_re = are_ref[...]
    a_im = aim_ref[...]
    dt = jnp.exp(ldt_ref[...])
    mag = jnp.exp(a_re * dt)
    ab_re = mag * jnp.cos(a_im * dt)
    ab_im = mag * jnp.sin(a_im * dt)
    den = a_re * a_re + a_im * a_im
    f_re = ((ab_re - 1.0) * a_re + ab_im * a_im) / den
    f_im = (ab_im * a_re - (ab_re - 1.0) * a_im) / den
    abre_ref[...] = ab_re
    abim_ref[...] = ab_im
    b_re = bre_ref[...]
    b_im = bim_ref[...]
    bbre_ref[...] = f_re[:, None, :] * b_re - f_im[:, None, :] * b_im
    bbim_ref[...] = f_re[:, None, :] * b_im + f_im[:, None, :] * b_re


def _s5_prep(a_re, a_im, log_dt, b_re, b_im):
    depth = a_re.shape[0]
    r = depth * SSM_G
    are = a_re.reshape(r, SSM_P)
    aim = a_im.reshape(r, SSM_P)
    ldt = jnp.broadcast_to(log_dt.reshape(r, 1), (r, SSM_P))
    bre = jnp.transpose(b_re, (0, 1, 3, 2)).reshape(r, SSM_GC, SSM_P)
    bim = jnp.transpose(b_im, (0, 1, 3, 2)).reshape(r, SSM_GC, SSM_P)
    outs = pl.pallas_call(
        _s5_prep_kernel,
        out_shape=(jax.ShapeDtypeStruct((r, SSM_P), F32), jax.ShapeDtypeStruct((r, SSM_P), F32),
                   jax.ShapeDtypeStruct((r, SSM_GC, SSM_P), F32), jax.ShapeDtypeStruct((r, SSM_GC, SSM_P), F32)),
    )(are, aim, ldt, bre, bim)
    ab_re, ab_im, bb_re, bb_im = outs
    ab_re = ab_re.reshape(depth, 1, SSM_N)
    ab_im = ab_im.reshape(depth, 1, SSM_N)
    eye = jnp.eye(SSM_G, dtype=F32)

    def blockdiag(bb):
        bb = bb.reshape(depth, SSM_G, SSM_GC, SSM_P)
        return (bb[:, :, :, None, :] * eye[None, :, None, :, None]).reshape(depth, BR, SSM_N)

    bbcat = jnp.concatenate([blockdiag(bb_re), blockdiag(bb_im)], axis=-1).astype(BF16)
    return ab_re, ab_im, bbcat


def _s5_cmat(c_re, c_im):
    depth = c_re.shape[0]
    eye = jnp.eye(SSM_G, dtype=F32)

    def blockdiag(c):
        ct = jnp.transpose(c, (0, 1, 3, 2))
        return (ct[:, :, :, None, :] * eye[None, :, None, :, None]).reshape(depth, SSM_N, BR)

    return jnp.concatenate([blockdiag(c_re), -blockdiag(c_im)], axis=1).astype(BF16)


def _inproj_kernel(*refs, prompt, tiles_per_batch):
    if prompt:
        (x_ref, g_ref, w_ref, wa2_ref, ba_ref, bf_ref, cos_ref, sin_ref, eall_ref, cq_ref, ck_ref,
         pret_ref, su_ref, pgla_ref, knat_ref, vnat_ref, lf_ref, fq_ref, fk_ref, fv_ref, carry_ref) = refs
    else:
        (x_ref, g_ref, w_ref, wa2_ref, ba_ref, bf_ref, cos_ref, sin_ref,
         pret_ref, su_ref, pgla_ref, knat_ref, vnat_ref, lf_ref, fqn_ref) = refs
    x = x_ref[...]
    tm = x.shape[0]
    h = _rms(x, g_ref[...]).astype(BF16)
    proj = _dot(h, w_ref[...])
    cos = cos_ref[...]
    sin = sin_ref[...]
    for base in (C_RQ, C_RK):
        x1 = proj[:, base:base + 128]
        x2 = proj[:, base + 128:base + 256]
        pret_ref[:, base:base + 128] = x1 * cos - x2 * sin
        pret_ref[:, base + 128:base + 256] = x1 * sin + x2 * cos
    pret_ref[:, C_RV:C_SU] = proj[:, C_RV:C_SU]
    su_ref[...] = proj[:, C_SU:C_GQ]
    small = proj[:, C_SM:C_SM + 128]
    la = _log_sigmoid(_dot(small.astype(BF16), wa2_ref[...]) + ba_ref[...]) * (1.0 / GLA_TAU)
    pgla_ref[:, G_V:G_Q] = proj[:, C_GV:C_FQ]
    pgla_ref[:, G_Q:G_K] = proj[:, C_GQ:C_GK]
    pgla_ref[:, G_K:G_LA] = proj[:, C_GK:C_GV] * (GLA_DK ** -0.5)
    pgla_ref[:, G_LA:G_LA + 128] = la
    lane = _iota((tm, 128), 1)
    lf = jnp.where(lane < HEADS, _log_sigmoid(small + bf_ref[...]), 0.0)
    lf_ref[...] = lf
    knat_ref[...] = proj[:, C_FK:C_FV]
    vnat_ref[...] = proj[:, C_FV:C_SM]
    if not prompt:
        fqn_ref[...] = proj[:, C_FQ:C_FK]
        return
    i = pl.program_id(0)

    @pl.when(i % tiles_per_batch == 0)
    def _():
        carry_ref[...] = jnp.zeros_like(carry_ref)

    tri = (_iota((tm, tm), 0) >= _iota((tm, tm), 1)).astype(BF16)
    c = _dot3_left(tri, lf) + carry_ref[0:1, :]
    carry_ref[...] = jnp.broadcast_to(c[tm - 1:tm, :], carry_ref.shape)
    chi, cmid, clo = _split3(c)
    c3 = (chi.astype(F32) + pltpu.roll(cmid.astype(F32), HEADS, 1)
          + pltpu.roll(clo.astype(F32), 2 * HEADS, 1)).astype(BF16)
    aug = _dot(c3, eall_ref[...])
    cq = cq_ref[...]
    ck = ck_ref[...]
    keep = lane < FOX_HD
    for hd in range(HEADS):
        lo = 128 * (hd // 2)
        bq = proj[:, C_FQ + lo:C_FQ + lo + 128]
        bk = proj[:, C_FK + lo:C_FK + lo + 128]
        bv = proj[:, C_FV + lo:C_FV + lo + 128]
        if hd % 2:
            bq = pltpu.roll(bq, FOX_HD, 1)
            bk = pltpu.roll(bk, FOX_HD, 1)
            bv = pltpu.roll(bv, FOX_HD, 1)
        fq_ref[hd] = (jnp.where(keep, bq, 0.0) + aug[:, 128 * hd:128 * hd + 128] + cq).astype(BF16)
        fk_ref[hd] = (jnp.where(keep, bk, 0.0) + aug[:, 128 * (HEADS + hd):128 * (HEADS + hd) + 128] + ck).astype(BF16)
        fv_ref[hd] = jnp.where(keep, bv, 0.0).astype(BF16)


def _inproj(x2d, g, w, wa2, ba, bfrow, cos, sin, consts, *, prompt, seq_len, tm):
    t = x2d.shape[0]
    nt = t // tm
    tpb = max(seq_len // tm, 1)
    row = lambda c: pl.BlockSpec((tm, c), lambda i: (i, 0))
    in_specs = [row(D_MODEL), _full((1, D_MODEL)), _full((D_MODEL, W_IN_COLS)), _full((128, 128)),
                _full((1, 128)), _full((1, 128)),
                pl.BlockSpec((tm, 128), lambda i: (i % tpb, 0)), pl.BlockSpec((tm, 128), lambda i: (i % tpb, 0))]
    args = [x2d, g, w, wa2, ba, bfrow, cos, sin]
    out_shape = [jax.ShapeDtypeStruct((t, 1024), F32), jax.ShapeDtypeStruct((t, BR), F32),
                 jax.ShapeDtypeStruct((t, 896), F32), jax.ShapeDtypeStruct((t, BR), F32),
                 jax.ShapeDtypeStruct((t, BR), F32), jax.ShapeDtypeStruct((t, 128), F32)]
    out_specs = [row(1024), row(BR), row(896), row(BR), row(BR), row(128)]
    scratch = []
    if prompt:
        eall, cq, ck = consts
        in_specs += [_full((128, 8 * 128)), _full((1, 128)), _full((1, 128))]
        args += [eall, cq, ck]
        hm = pl.BlockSpec((HEADS, tm, 128), lambda i: (0, i, 0))
        out_shape += [jax.ShapeDtypeStruct((HEADS, t, 128), BF16)] * 3
        out_specs += [hm, hm, hm]
        scratch = [pltpu.VMEM((SUBLANES, 128), F32)]
    else:
        out_shape += [jax.ShapeDtypeStruct((t, BR), F32)]
        out_specs += [row(BR)]
    return pl.pallas_call(
        functools.partial(_inproj_kernel, prompt=prompt, tiles_per_batch=tpb),
        grid=(nt,), in_specs=in_specs, out_specs=out_specs, out_shape=out_shape,
        scratch_shapes=scratch, compiler_params=_cparams(("arbitrary",)),
    )(*args)


def _ret_log_gamma():
    return [math.log(1.0 - 2.0 ** (-5.0 - h)) for h in range(HEADS)]


def _lane_select(idx, values):
    out = jnp.zeros(idx.shape, F32)
    for h, v in enumerate(values):
        out = jnp.where(idx == h, v, out)
    return out


def _ret_kernel(q_ref, k_ref, v_ref, g_ref, gn_ref, o_ref, s_ref, S, *, nchunk):
    j = pl.program_id(1)
    c = RET_CHUNK

    @pl.when(j == 0)
    def _():
        S[...] = jnp.zeros_like(S)

    lg = _ret_log_gamma()
    lane = _iota((1, BR), 1)
    head_qk = (lane & 127) >> 5
    head_v = lane >> 6
    lg_lane = _lane_select(head_qk, lg)
    t = _iota((c, 1), 0).astype(F32)
    gq = jnp.exp((t + 1.0) * lg_lane)
    gk = jnp.exp((c - 1.0 - t) * lg_lane)
    rowi = _iota((BR, 1), 0)
    head_row = (rowi & 127) >> 5
    gam = jnp.exp(float(c) * _lane_select(head_row, lg))
    bd = head_row == head_v
    dt = (_iota((c, c), 0) - _iota((c, c), 1))
    dmat = jnp.concatenate(
        [jnp.where(dt >= 0, jnp.exp(dt.astype(F32) * lg[h]), 0.0) for h in range(HEADS)], axis=0)
    avg = _head_avg(BR, RET_DK)
    gn = gn_ref[...]
    for ci in range(nchunk):
        rows = slice(ci * c, (ci + 1) * c)
        q = q_ref[rows, :]
        k = k_ref[rows, :]
        vb = v_ref[rows, :].astype(BF16)
        qs = jnp.concatenate([jnp.where(head_qk == h, q, 0.0) for h in range(HEADS)], axis=0).astype(BF16)
        sc = (_dot_nt(qs, k.astype(BF16)) * dmat).astype(BF16)
        pv = _dot(sc, vb)
        o = _dot((q * gq).astype(BF16), S[...].astype(BF16))
        for h in range(HEADS):
            o = o + jnp.where(head_v == h, pv[h * c:(h + 1) * c, :], 0.0)
        S[...] = gam * S[...] + jnp.where(bd, _dot_tn((k * gk).astype(BF16), vb), 0.0)
        oc = o - _dot2_right(o, avg)
        on = oc * lax.rsqrt(_dot2_right(oc * oc, avg) + EPS)
        gt = g_ref[rows, :]
        o_ref[rows, :] = (on * gn * (gt * _sigmoid(gt))).astype(BF16)

    @pl.when(j == pl.num_programs(1) - 1)
    def _():
        s_ref[0] = S[...]


def _retention(pret, gn, *, batch, seq_len, tc):
    t = pret.shape[0]
    nj = seq_len // tc
    col = lambda cb: pl.BlockSpec((tc, BR), lambda b, j, _cb=cb: (b * nj + j, _cb))
    return pl.pallas_call(
        functools.partial(_ret_kernel, nchunk=tc // RET_CHUNK),
        grid=(batch, nj),
        in_specs=[col(0), col(1), col(2), col(3), _full((1, BR))],
        out_specs=[pl.BlockSpec((tc, BR), lambda b, j: (b * nj + j, 0)),
                   pl.BlockSpec((1, BR, BR), lambda b, j: (b, 0, 0))],
        out_shape=[jax.ShapeDtypeStruct((t, BR), BF16), jax.ShapeDtypeStruct((batch, BR, BR), F32)],
        scratch_shapes=[pltpu.VMEM((BR, BR), F32)],
        compiler_params=_cparams(("arbitrary", "arbitrary")),
    )(pret, pret, pret, pret, gn)


def _cmul(ar, ai, br, bi):
    return ar * br - ai * bi, ar * bi + ai * br


def _s5_kernel(u_ref, abre_ref, abim_ref, bb_ref, cc_ref, d_ref, wg_ref, bg_ref,
               y_ref, sre_ref, sim_ref, xr_ref, xi_ref, car_ref):
    j = pl.program_id(1)
    tm = u_ref.shape[0]

    @pl.when(j == 0)
    def _():
        car_ref[...] = jnp.zeros_like(car_ref)

    a1r = abre_ref[...]
    a1i = abim_ref[...]
    u = u_ref[...]
    bu = _dot(u.astype(BF16), bb_ref[...])
    xr = bu[:, :SSM_N]
    xi = bu[:, SSM_N:]
    row8 = _iota((tm, 1), 0) & 7
    pr, pi_ = a1r, a1i
    for k in (1, 2, 4):
        ok = row8 >= k
        sr = jnp.where(ok, pltpu.roll(xr, k, 0), 0.0)
        si = jnp.where(ok, pltpu.roll(xi, k, 0), 0.0)
        dr, di = _cmul(pr, pi_, sr, si)
        xr = xr + dr
        xi = xi + di
        pr, pi_ = _cmul(pr, pi_, pr, pi_)
    r8 = _iota((SUBLANES, 1), 0)
    tr = jnp.broadcast_to(a1r, (SUBLANES, SSM_N))
    ti = jnp.broadcast_to(a1i, (SUBLANES, SSM_N))
    qr, qi = a1r, a1i
    for bit in range(3):
        nr, ni = _cmul(tr, ti, qr, qi)
        use = ((r8 >> bit) & 1) == 1
        tr = jnp.where(use, nr, tr)
        ti = jnp.where(use, ni, ti)
        qr, qi = _cmul(qr, qi, qr, qi)
    xr_ref[...] = xr
    xi_ref[...] = xi

    def body(g, carry):
        cr, ci = carry
        rows = pl.ds(pl.multiple_of(g * SUBLANES, SUBLANES), SUBLANES)
        dr, di = _cmul(tr, ti, cr, ci)
        r2 = xr_ref[rows, :] + dr
        i2 = xi_ref[rows, :] + di
        xr_ref[rows, :] = r2
        xi_ref[rows, :] = i2
        return r2[SUBLANES - 1:SUBLANES, :], i2[SUBLANES - 1:SUBLANES, :]

    cr, ci = lax.fori_loop(0, tm // SUBLANES, body, (car_ref[0:1, :], car_ref[1:2, :]))
    car_ref[0:1, :] = cr
    car_ref[1:2, :] = ci
    sre_ref[0] = cr
    sim_ref[0] = ci
    xcat = jnp.concatenate([xr_ref[...].astype(BF16), xi_ref[...].astype(BF16)], axis=1)
    y = _dot(xcat, cc_ref[...]) + d_ref[...] * u
    yg = _gelu_tanh(y)
    y_ref[...] = (yg * _sigmoid(_dot(yg.astype(BF16), wg_ref[...]) + bg_ref[...])).astype(BF16)


def _s5(su, ab_re, ab_im, bbcat, cccat, d_skip, w_glu, b_glu, *, batch, seq_len, tm):
    t = su.shape[0]
    nj = seq_len // tm
    return pl.pallas_call(
        _s5_kernel,
        grid=(batch, nj),
        in_specs=[pl.BlockSpec((tm, BR), lambda b, j: (b * nj + j, 0)), _full((1, SSM_N)), _full((1, SSM_N)),
                  _full((BR, 2 * SSM_N)), _full((2 * SSM_N, BR)), _full((1, BR)), _full((BR, BR)), _full((1, BR))],
        out_specs=[pl.BlockSpec((tm, BR), lambda b, j: (b * nj + j, 0)),
                   pl.BlockSpec((1, 1, SSM_N), lambda b, j: (b, 0, 0)),
                   pl.BlockSpec((1, 1, SSM_N), lambda b, j: (b, 0, 0))],
        out_shape=[jax.ShapeDtypeStruct((t, BR), BF16), jax.ShapeDtypeStruct((batch, 1, SSM_N), F32),
                   jax.ShapeDtypeStruct((batch, 1, SSM_N), F32)],
        scratch_shapes=[pltpu.VMEM((tm, SSM_N), F32), pltpu.VMEM((tm, SSM_N), F32), pltpu.VMEM((SUBLANES, SSM_N), F32)],
        compiler_params=_cparams(("arbitrary", "arbitrary")),
    )(su, ab_re, ab_im, bbcat, cccat, d_skip, w_glu, b_glu)


def _gla_kernel(q_ref, k_ref, la_ref, v_ref, g_ref, gn_ref, o_ref, s_ref, ST, oi_ref, qt_ref, kh_ref, el_ref):
    j = pl.program_id(1)
    tc = q_ref.shape[0]
    c = GLA_CHUNK
    nchunk = tc // c

    @pl.when(j == 0)
    def _():
        ST[...] = jnp.zeros_like(ST)

    r = _iota((tc, tc), 0)
    s = _iota((tc, tc), 1)
    same = (r // c) == (s // c)
    tri = (same & (s <= r)).astype(BF16)
    ones_blk = same.astype(BF16)
    la = la_ref[...]
    hi, mid, lo = _split3(la)
    b = _dot(tri, hi) + _dot(tri, mid) + _dot(tri, lo)
    bl = _dot(ones_blk, hi) + _dot(ones_blk, mid) + _dot(ones_blk, lo)
    q = q_ref[...]
    k = k_ref[...]
    qt = q * jnp.exp(b)
    kt = k * jnp.exp(jnp.minimum(-b, EXP_CLAMP))
    qt_ref[...] = qt
    kh_ref[...] = k * jnp.exp(bl - b)
    el_ref[...] = jnp.exp(bl)
    lane_k = _iota((1, GLA_KW), 1) >> 5
    lane_v = _iota((1, BR), 1) >> 6
    qs = jnp.concatenate([jnp.where(lane_k == h, qt, 0.0) for h in range(HEADS)], axis=0).astype(BF16)
    sc = _dot_nt(qs, kt.astype(BF16))
    keep = jnp.concatenate([same & (s <= r)] * HEADS, axis=0)
    sc = jnp.where(keep, sc, 0.0).astype(BF16)
    vb = v_ref[...].astype(BF16)
    pv = _dot(sc, vb)
    o = jnp.zeros((tc, BR), F32)
    for h in range(HEADS):
        o = o + jnp.where(lane_v == h, pv[h * tc:(h + 1) * tc, :], 0.0)
    oi_ref[...] = o
    bd = (_iota((BR, 1), 0) >> 6) == lane_k

    def body(ci, carry):
        rows = pl.ds(pl.multiple_of(ci * c, c), c)
        st = ST[...]
        oi_ref[rows, :] = oi_ref[rows, :] + _dot_nt(qt_ref[rows, :].astype(BF16), st.astype(BF16))
        upd = _dot_tn(v_ref[rows, :].astype(BF16), kh_ref[rows, :].astype(BF16))
        last = pl.ds(pl.multiple_of(ci * c, c) + (c - 1), 1)
        ST[...] = el_ref[last, :] * st + jnp.where(bd, upd, 0.0)
        return carry

    lax.fori_loop(0, nchunk, body, 0)
    o = oi_ref[...]
    avg = _head_avg(BR, BR // HEADS)
    on = o * lax.rsqrt(_dot2_right(o * o, avg) + EPS)
    gt = g_ref[...]
    o_ref[...] = (on * gn_ref[...] * (gt * _sigmoid(gt))).astype(BF16)

    @pl.when(j == pl.num_programs(1) - 1)
    def _():
        s_ref[0] = ST[...]


def _gla(pgla, gn, *, batch, seq_len, tc):
    t = pgla.shape[0]
    nj = seq_len // tc
    c128 = lambda cb: pl.BlockSpec((tc, 128), lambda b, j, _cb=cb: (b * nj + j, _cb))
    c256 = lambda cb: pl.BlockSpec((tc, BR), lambda b, j, _cb=cb: (b * nj + j, _cb))
    return pl.pallas_call(
        _gla_kernel,
        grid=(batch, nj),
        in_specs=[c128(G_Q // 128), c128(G_K // 128), c128(G_LA // 128), c256(G_V // BR), c256(G_G // BR),
                  _full((1, BR))],
        out_specs=[pl.BlockSpec((tc, BR), lambda b, j: (b * nj + j, 0)),
                   pl.BlockSpec((1, BR, GLA_KW), lambda b, j: (b, 0, 0))],
        out_shape=[jax.ShapeDtypeStruct((t, BR), BF16), jax.ShapeDtypeStruct((batch, BR, GLA_KW), F32)],
        scratch_shapes=[pltpu.VMEM((BR, GLA_KW), F32), pltpu.VMEM((tc, BR), F32), pltpu.VMEM((tc, GLA_KW), F32),
                        pltpu.VMEM((tc, GLA_KW), F32), pltpu.VMEM((tc, GLA_KW), F32)],
        compiler_params=_cparams(("arbitrary", "arbitrary")),
    )(pgla, pgla, pgla, pgla, pgla, gn)


NEG = -1e30


def _fox_kernel(qi_tab, kj_tab, q_ref, k_ref, v_ref, o_ref, m_ref, l_ref, acc_ref):
    p = pl.program_id(1)
    qi = qi_tab[p]
    kj = kj_tab[p]
    tq = q_ref.shape[1]

    @pl.when(kj == 0)
    def _():
        m_ref[...] = jnp.full_like(m_ref, NEG)
        l_ref[...] = jnp.zeros_like(l_ref)
        acc_ref[...] = jnp.zeros_like(acc_ref)

    causal = (kj * tq + _iota((tq, tq), 1)) <= (qi * tq + _iota((tq, tq), 0))
    for h in range(HEADS):
        s = jnp.where(causal, _dot_nt(q_ref[h], k_ref[h]), NEG)
        m_prev = m_ref[h][:, 0:1]
        m_new = jnp.maximum(m_prev, jnp.max(s, axis=1, keepdims=True))
        alpha = jnp.exp(m_prev - m_new)
        pe = jnp.exp(s - m_new)
        l_ref[h] = jnp.broadcast_to(alpha * l_ref[h][:, 0:1] + jnp.sum(pe, axis=1, keepdims=True), (tq, 128))
        acc_ref[h] = alpha * acc_ref[h] + _dot(pe.astype(BF16), v_ref[h])
        m_ref[h] = jnp.broadcast_to(m_new, (tq, 128))

    @pl.when(kj == qi)
    def _():
        for pair in range(HEADS // 2):
            oe = acc_ref[2 * pair] / l_ref[2 * pair][:, 0:1]
            oo = acc_ref[2 * pair + 1] / l_ref[2 * pair + 1][:, 0:1]
            o_ref[:, 128 * pair:128 * pair + 128] = (oe + pltpu.roll(oo, FOX_HD, 1)).astype(BF16)


def _fox_prompt(fq, fk, fv, *, batch, seq_len, tq):
    t = fq.shape[1]
    nq = seq_len // tq
    pairs = [(i, j) for i in range(nq) for j in range(i + 1)]
    qi_tab = jnp.asarray([p[0] for p in pairs], jnp.int32)
    kj_tab = jnp.asarray([p[1] for p in pairs], jnp.int32)
    grid_spec = pltpu.PrefetchScalarGridSpec(
        num_scalar_prefetch=2, grid=(batch, len(pairs)),
        in_specs=[pl.BlockSpec((HEADS, tq, 128), lambda b, p, qt, kt: (0, b * nq + qt[p], 0)),
                  pl.BlockSpec((HEADS, tq, 128), lambda b, p, qt, kt: (0, b * nq + kt[p], 0)),
                  pl.BlockSpec((HEADS, tq, 128), lambda b, p, qt, kt: (0, b * nq + kt[p], 0))],
        out_specs=pl.BlockSpec((tq, BR), lambda b, p, qt, kt: (b * nq + qt[p], 0)),
        scratch_shapes=[pltpu.VMEM((HEADS, tq, 128), F32)] * 3)
    return pl.pallas_call(
        _fox_kernel, grid_spec=grid_spec, out_shape=jax.ShapeDtypeStruct((t, BR), BF16),
        compiler_params=_cparams(("arbitrary", "arbitrary")),
    )(qi_tab, kj_tab, fq, fk, fv)


def _merge_kernel(x_ref, o0_ref, o1_ref, o2_ref, o3_ref, g_ref, wg_ref, bg_ref, wb_ref, wo_ref, out_ref):
    x = x_ref[...]
    h = _rms(x, g_ref[...]).astype(BF16)
    merged = None
    for n, o_ref in enumerate((o0_ref, o1_ref, o2_ref, o3_ref)):
        cols = slice(n * D_MODEL, (n + 1) * D_MODEL)
        gate = _sigmoid(_dot(h, wg_ref[:, cols]) + bg_ref[:, cols])
        term = gate * _dot(o_ref[...], wb_ref[n])
        merged = term if merged is None else merged + term
    out_ref[...] = x + _dot(merged.astype(BF16), wo_ref[...])


def _merge(x2d, branches, g, wg, bg, wb, wo, *, tm):
    t = x2d.shape[0]
    row = lambda c: pl.BlockSpec((tm, c), lambda i: (i, 0))
    return pl.pallas_call(
        _merge_kernel, grid=(t // tm,),
        in_specs=[row(D_MODEL), row(BR), row(BR), row(BR), row(BR), _full((1, D_MODEL)),
                  _full((D_MODEL, N_BRANCH * D_MODEL)), _full((1, N_BRANCH * D_MODEL)),
                  _full((N_BRANCH, BR, D_MODEL)), _full((D_MODEL, D_MODEL))],
        out_specs=row(D_MODEL), out_shape=jax.ShapeDtypeStruct((t, D_MODEL), F32),
        compiler_params=_cparams(("arbitrary",)),
    )(x2d, *branches, g, wg, bg, wb, wo)


def _ffn_kernel(*refs, prompt, tiles_per_batch):
    if prompt:
        x_ref, g_ref, wu_ref, cw_ref, cb_ref, wd_ref, out_ref, tail_ref, prev_ref = refs
    else:
        x_ref, g_ref, wu_ref, cw_ref, cb_ref, wd_ref, c0_ref, c1_ref, out_ref, a_ref = refs
    x = x_ref[...]
    tm = x.shape[0]
    h = _rms(x, g_ref[...]).astype(BF16)
    if prompt:
        i = pl.program_id(0)

        @pl.when(i % tiles_per_batch == 0)
        def _():
            prev_ref[...] = jnp.zeros_like(prev_ref)

        rowid = _iota((tm, 1), 0)
    y = x
    for ch in range(D_FF // FF_CHUNK):
        cols = slice(ch * FF_CHUNK, (ch + 1) * FF_CHUNK)
        gcols = slice(D_FF + ch * FF_CHUNK, D_FF + (ch + 1) * FF_CHUNK)
        a = _dot(h, wu_ref[:, cols])
        gate = _dot(h, wu_ref[:, gcols])
        if prompt:
            p1 = prev_ref[SUBLANES - 1:SUBLANES, cols]
            p2 = prev_ref[SUBLANES - 2:SUBLANES - 1, cols]
            a1 = jnp.where(rowid == 0, p1, pltpu.roll(a, 1, 0))
            a2 = jnp.where(rowid == 0, p2, jnp.where(rowid == 1, p1, pltpu.roll(a, 2, 0)))
            prev_ref[:, cols] = a[tm - SUBLANES:tm, :]
            tail_ref[0, :, cols] = a[tm - SUBLANES:tm, :]
        else:
            a1 = c1_ref[:, cols]
            a2 = c0_ref[:, cols]
            a_ref[:, cols] = a
        conv = cb_ref[:, cols] + cw_ref[0:1, cols] * a2 + cw_ref[1:2, cols] * a1 + cw_ref[2:3, cols] * a
        act = (_gelu_tanh(conv) * gate).astype(BF16)
        y = y + _dot(act, wd_ref[cols, :])
    out_ref[...] = y


def _ffn(x2d, g, wu, cw, cb, wd, conv0=None, *, prompt, batch, seq_len, tm):
    t = x2d.shape[0]
    tpb = max(seq_len // tm, 1)
    row = lambda c: pl.BlockSpec((tm, c), lambda i: (i, 0))
    in_specs = [row(D_MODEL), _full((1, D_MODEL)), _full((D_MODEL, 2 * D_FF)), _full((CONV_W, D_FF)),
                _full((1, D_FF)), _full((D_FF, D_MODEL))]
    args = [x2d, g, wu, cw, cb, wd]
    if prompt:
        out_shape = [jax.ShapeDtypeStruct((t, D_MODEL), F32), jax.ShapeDtypeStruct((batch, SUBLANES, D_FF), F32)]
        out_specs = [row(D_MODEL), pl.BlockSpec((1, SUBLANES, D_FF), lambda i: (i // tpb, 0, 0))]
        scratch = [pltpu.VMEM((SUBLANES, D_FF), F32)]
    else:
        in_specs += [row(D_FF), row(D_FF)]
        args += [conv0[:, 0, :], conv0[:, 1, :]]
        out_shape = [jax.ShapeDtypeStruct((t, D_MODEL), F32), jax.ShapeDtypeStruct((t, D_FF), F32)]
        out_specs = [row(D_MODEL), row(D_FF)]
        scratch = []
    return pl.pallas_call(
        functools.partial(_ffn_kernel, prompt=prompt, tiles_per_batch=tpb),
        grid=(t // tm,), in_specs=in_specs, out_specs=out_specs, out_shape=out_shape,
        scratch_shapes=scratch, compiler_params=_cparams(("arbitrary",)),
    )(*args)


def _norm_kernel(x_ref, g_ref, o_ref):
    o_ref[...] = _rms(x_ref[...], g_ref[...])


def _final_norm(x2d, g, *, tm):
    t = x2d.shape[0]
    row = pl.BlockSpec((tm, D_MODEL), lambda i: (i, 0))
    return pl.pallas_call(
        _norm_kernel, grid=(t // tm,), in_specs=[row, _full((1, D_MODEL))], out_specs=row,
        out_shape=jax.ShapeDtypeStruct((t, D_MODEL), F32), compiler_params=_cparams(("arbitrary",)),
    )(x2d, g)


def _sample_mix_kernel(rq_ref, rk_ref, rv_ref, rg_ref, rs_ref, rgn_ref,
                       gq_ref, gk_ref, gla_ref, gv_ref, gg_ref, gs_ref, ggn_ref,
                       u_ref, sre_ref, sim_ref, abre_ref, abim_ref, bb_ref, cc_ref, d_ref, wg_ref, bg_ref,
                       oret_ref, rsn_ref, ogla_ref, gsn_ref, ossm_ref, sren_ref, simn_ref):
    lg = _ret_log_gamma()
    v = rv_ref[...]
    for h in range(HEADS):
        sn = math.exp(lg[h]) * rs_ref[:, h] + rk_ref[:, h] * v[:, h]
        rsn_ref[:, h] = sn
        o = jnp.sum(rq_ref[:, h] * sn, axis=1)
        oc = o - jnp.mean(o, axis=-1, keepdims=True)
        on = oc * lax.rsqrt(jnp.mean(oc * oc, axis=-1, keepdims=True) + EPS)
        gt = rg_ref[:, h]
        oret_ref[:, h] = on * rgn_ref[h] * (gt * _sigmoid(gt))
    gv = gv_ref[...]
    for h in range(HEADS):
        sn = jnp.exp(gla_ref[:, h]) * gs_ref[:, h] + gk_ref[:, h] * gv[:, h]
        gsn_ref[:, h] = sn
        o = jnp.sum(gq_ref[:, h] * sn, axis=1)
        on = o * lax.rsqrt(jnp.mean(o * o, axis=-1, keepdims=True) + EPS)
        gt = gg_ref[:, h]
        ogla_ref[:, h] = on * ggn_ref[h] * (gt * _sigmoid(gt))
    u = u_ref[...]
    bu = _dot(u.astype(BF16), bb_ref[...])
    dr, di = _cmul(abre_ref[...], abim_ref[...], sre_ref[...], sim_ref[...])
    xr = bu[:, :SSM_N] + dr
    xi = bu[:, SSM_N:] + di
    sren_ref[...] = xr
    simn_ref[...] = xi
    xcat = jnp.concatenate([xr.astype(BF16), xi.astype(BF16)], axis=1)
    y = _dot(xcat, cc_ref[...]) + d_ref[...] * u
    yg = _gelu_tanh(y)
    ossm_ref[...] = yg * _sigmoid(_dot(yg.astype(BF16), wg_ref[...]) + bg_ref[...])


def _sample_mix(pret, pgla, su, st_ret, st_gla, st_re, st_im, rgn, ggn, ab_re, ab_im, bbcat, cccat, d_skip, w_glu, b_glu):
    n = pret.shape[0]

    def unpack_qk(a):
        return jnp.transpose(a.reshape(n, 2, HEADS, 32), (0, 2, 1, 3)).reshape(n, HEADS, RET_DK, 1)

    rq = unpack_qk(pret[:, 0:256])
    rk = unpack_qk(pret[:, 256:512])
    rv = pret[:, 512:768].reshape(n, HEADS, 1, 64)
    rg = pret[:, 768:1024].reshape(n, HEADS, 64)
    gq = pgla[:, G_Q:G_K].reshape(n, HEADS, GLA_DK, 1)
    gk = pgla[:, G_K:G_LA].reshape(n, HEADS, GLA_DK, 1)
    gla = pgla[:, G_LA:G_LA + 128].reshape(n, HEADS, GLA_DK, 1)
    gv = pgla[:, G_V:G_G].reshape(n, HEADS, 1, 64)
    gg = pgla[:, G_G:G_Q].reshape(n, HEADS, 64)
    out_shape = [jax.ShapeDtypeStruct((n, HEADS, 64), F32), jax.ShapeDtypeStruct(st_ret.shape, F32),
                 jax.ShapeDtypeStruct((n, HEADS, 64), F32), jax.ShapeDtypeStruct(st_gla.shape, F32),
                 jax.ShapeDtypeStruct((n, BR), F32), jax.ShapeDtypeStruct((n, SSM_N), F32),
                 jax.ShapeDtypeStruct((n, SSM_N), F32)]
    outs = pl.pallas_call(
        _sample_mix_kernel, out_shape=out_shape,
        compiler_params=pltpu.CompilerParams(vmem_limit_bytes=VMEM_LIMIT),
    )(rq, rk, rv, rg, st_ret, rgn.reshape(HEADS, 1, 64), gq, gk, gla, gv, gg, st_gla, ggn.reshape(HEADS, 1, 64),
      su, st_re.reshape(n, SSM_N), st_im.reshape(n, SSM_N), ab_re, ab_im, bbcat, cccat, d_skip, w_glu, b_glu)
    o_ret, ret_new, o_gla, gla_new, o_ssm, re_new, im_new = outs
    return (o_ret.reshape(n, BR), o_ssm, o_gla.reshape(n, BR), ret_new, gla_new,
            re_new.reshape(n, SSM_G, SSM_P), im_new.reshape(n, SSM_G, SSM_P))


def _fox_sample_kernel(*refs, npp):
    pt_ref = refs[0]
    q_ref, kn_ref, vn_ref, lfn_ref = refs[1:5]
    k_refs = refs[5:5 + npp]
    v_refs = refs[5 + npp:5 + 2 * npp]
    lf_refs = refs[5 + 2 * npp:5 + 3 * npp]
    o_ref, m_ref, l_ref, acc_ref, car_ref = refs[5 + 3 * npp:]
    p = pl.program_id(1)
    hmask = _iota((SUBLANES, BR), 0) == (_iota((SUBLANES, BR), 1) >> 6)
    qblk = jnp.where(hmask, q_ref[0], 0.0)

    @pl.when(p == 0)
    def _():
        m_ref[...] = jnp.broadcast_to(jnp.sum(qblk * kn_ref[0], axis=1, keepdims=True), m_ref.shape)
        l_ref[...] = jnp.ones_like(l_ref)
        acc_ref[...] = jnp.broadcast_to(vn_ref[0], acc_ref.shape)
        car_ref[...] = jnp.broadcast_to(lfn_ref[0], car_ref.shape)

    upper = (_iota((PAGE, PAGE), 0) > _iota((PAGE, PAGE), 1)).astype(BF16)
    qb = qblk.astype(BF16)
    m = m_ref[:, 0:1]
    l = l_ref[:, 0:1]
    acc = acc_ref[...]
    car = car_ref[:, 0:1]
    for i in range(npp):
        lf = lf_refs[i][0]
        bias = _dot3_right(lf, upper) + car
        s = _dot_nt(qb, k_refs[i][0].astype(BF16)) + bias
        m_new = jnp.maximum(m, jnp.max(s, axis=1, keepdims=True))
        alpha = jnp.exp(m - m_new)
        pe = jnp.exp(s - m_new)
        l = alpha * l + jnp.sum(pe, axis=1, keepdims=True)
        acc = alpha * acc + _dot(pe.astype(BF16), v_refs[i][0].astype(BF16))
        m = m_new
        car = car + jnp.sum(lf, axis=1, keepdims=True)
    m_ref[...] = jnp.broadcast_to(m, m_ref.shape)
    l_ref[...] = jnp.broadcast_to(l, l_ref.shape)
    acc_ref[...] = acc
    car_ref[...] = jnp.broadcast_to(car, car_ref.shape)

    @pl.when(p == pl.num_programs(1) - 1)
    def _():
        o_ref[0] = jnp.sum(jnp.where(hmask, acc / l, 0.0), axis=0, keepdims=True)


def _fox_sample(page_table, fq, kn, vn, lfn, cache_k, cache_v, cache_lf):
    n, n_pages = page_table.shape
    npp = PAGES_PER_STEP
    while n_pages % npp:
        npp //= 2
    steps = n_pages // npp
    pt = page_table.reshape(-1)

    def page_spec(shape, i):
        return pl.BlockSpec(shape, lambda b, p, pt_, _i=i: (pt_[b * n_pages + n_pages - 1 - (p * npp + _i)], 0, 0))

    row3 = lambda c: pl.BlockSpec((1, 1, c), lambda b, p, pt_: (b, 0, 0))
    in_specs = ([row3(BR), row3(BR), row3(BR), pl.BlockSpec((1, SUBLANES, 1), lambda b, p, pt_: (b, 0, 0))]
                + [page_spec((1, PAGE, BR), i) for i in range(npp)]
                + [page_spec((1, PAGE, BR), i) for i in range(npp)]
                + [page_spec((1, SUBLANES, PAGE), i) for i in range(npp)])
    grid_spec = pltpu.PrefetchScalarGridSpec(
        num_scalar_prefetch=1, grid=(n, steps), in_specs=in_specs, out_specs=row3(BR),
        scratch_shapes=[pltpu.VMEM((SUBLANES, 128), F32), pltpu.VMEM((SUBLANES, 128), F32),
                        pltpu.VMEM((SUBLANES, BR), F32), pltpu.VMEM((SUBLANES, 128), F32)])
    lfn8 = jnp.pad(lfn, ((0, 0), (0, SUBLANES - HEADS)))[:, :, None]
    out = pl.pallas_call(
        functools.partial(_fox_sample_kernel, npp=npp), grid_spec=grid_spec,
        out_shape=jax.ShapeDtypeStruct((n, 1, BR), F32),
        compiler_params=_cparams(("arbitrary", "arbitrary")),
    )(pt, fq[:, None, :], kn[:, None, :], vn[:, None, :], lfn8,
      *([cache_k] * npp), *([cache_v] * npp), *([cache_lf] * npp))
    return out.reshape(n, BR)


def _pack_w_in(w):
    def rope_pack(blk):
        d = blk.shape[0]
        b4 = blk.reshape(d, D_MODEL, HEADS, 2, 32)
        return jnp.transpose(b4, (0, 1, 3, 2, 4)).reshape(d, D_MODEL, 256)

    depth = w.shape[0]
    small = jnp.zeros((depth, D_MODEL, 128), w.dtype)
    small = small.at[:, :, 0:HEADS].set(w[:, :, N_FF:N_FF + HEADS])
    small = small.at[:, :, HEADS:HEADS + GLA_RANK].set(w[:, :, N_GA:N_GA + GLA_RANK])
    parts = [rope_pack(w[:, :, N_RQ:N_RK]), rope_pack(w[:, :, N_RK:N_RV]) * (RET_DK ** -0.5),
             w[:, :, N_RV:N_GQ],
             w[:, :, N_GQ:N_GA],
             w[:, :, N_FQ:N_FK] * (FOX_HD ** -0.5), w[:, :, N_FK:N_FF], small]
    return jnp.concatenate(parts, axis=-1).astype(BF16)


def _rope_tables(pos):
    inv = ROPE_BASE ** (-jnp.arange(32, dtype=F32) * 2.0 / RET_DK)
    ang = pos.astype(F32)[:, None] * inv[None, :]
    return jnp.tile(jnp.cos(ang), (1, HEADS)), jnp.tile(jnp.sin(ang), (1, HEADS))


def _fox_consts():
    eall = np.zeros((128, 8 * 128), np.float32)
    for h in range(HEADS):
        for part in range(3):
            eall[part * HEADS + h, 128 * h + FOX_HD + part] = 1.0
            eall[part * HEADS + h, 128 * (HEADS + h) + FOX_HD + 3 + part] = -1.0
    cq = np.zeros((1, 128), np.float32)
    ck = np.zeros((1, 128), np.float32)
    cq[0, FOX_HD + 3:FOX_HD + 6] = 1.0
    ck[0, FOX_HD:FOX_HD + 3] = 1.0
    return jnp.asarray(eall, BF16), jnp.asarray(cq), jnp.asarray(ck)


def _diag_blocks(s, nh, dk, dv):
    b = s.shape[0]
    s5 = s.reshape(b, nh, dk, nh, dv)
    idx = jnp.arange(nh)
    return jnp.transpose(s5[:, idx, :, idx, :], (1, 0, 2, 3))


def kernel(x_prompt, x_sample, cache_k, cache_v, cache_logf, state_ret, state_ssm_re, state_ssm_im, state_gla, state_conv, page_table, norm1_g, w_in, w_mgate, b_mgate, ret_norm_g, ssm_a_re, ssm_a_im, ssm_log_dt, ssm_b_re, ssm_b_im, ssm_c_re, ssm_c_im, ssm_d, ssm_w_glu, ssm_b_glu, gla_w_a2, gla_b_a, gla_norm_g, fox_b_f, w_branch, w_out, norm2_g, ffn_w_up, ffn_conv_w, ffn_conv_b, ffn_w_down, final_norm_g):
    depth = w_in.shape[0]
    B, L, _ = x_prompt.shape
    NS = x_sample.shape[0]
    n_pages = page_table.shape[1]
    past = n_pages * PAGE
    T = B * L
    tm = min(512, L)
    n_pool = cache_k.shape[1]

    w_in_p = _pack_w_in(w_in)
    wa2 = jnp.zeros((depth, 128, 128), F32).at[:, HEADS:HEADS + GLA_RANK, :].set(gla_w_a2).astype(BF16)
    bfrow = jnp.zeros((depth, 1, 128), F32).at[:, 0, 0:HEADS].set(fox_b_f)
    wg_b = w_mgate.astype(BF16)
    wb_b = w_branch.astype(BF16)
    wo_b = w_out.astype(BF16)
    wu_b = ffn_w_up.astype(BF16)
    wd_b = ffn_w_down.astype(BF16)
    wglu_b = ssm_w_glu.astype(BF16)
    ab_re, ab_im, bbcat = _s5_prep(ssm_a_re, ssm_a_im, ssm_log_dt, ssm_b_re, ssm_b_im)
    cccat = _s5_cmat(ssm_c_re, ssm_c_im)
    cos_p, sin_p = _rope_tables(jnp.arange(L))
    cos_s, sin_s = _rope_tables(jnp.full((NS,), past))
    fox_consts = _fox_consts()
    ck3 = cache_k.reshape(depth, n_pool, PAGE, BR)
    cv3 = cache_v.reshape(depth, n_pool, PAGE, BR)
    clf3 = jnp.pad(jnp.transpose(cache_logf, (0, 1, 3, 2)), ((0, 0), (0, 0), (0, SUBLANES - HEADS), (0, 0)))

    row = lambda a, l: a[l][None, :]
    xp = x_prompt.reshape(T, D_MODEL)
    xs = x_sample.reshape(NS, D_MODEL)
    outs_p, outs_s = [], []
    for l in range(depth):
        (pret, su, pgla, knat, vnat, lf, fq, fk, fv) = _inproj(
            xp, row(norm1_g, l), w_in_p[l], wa2[l], row(gla_b_a, l), bfrow[l], cos_p, sin_p, fox_consts,
            prompt=True, seq_len=L, tm=tm)
        o_ret, s_ret = _retention(pret, row(ret_norm_g, l), batch=B, seq_len=L, tc=tm)
        o_ssm, s_re, s_im = _s5(su, ab_re[l], ab_im[l], bbcat[l], cccat[l], row(ssm_d, l), wglu_b[l],
                                row(ssm_b_glu, l), batch=B, seq_len=L, tm=tm)
        o_gla, s_gla = _gla(pgla, row(gla_norm_g, l), batch=B, seq_len=L, tc=tm)
        o_fox = _fox_prompt(fq, fk, fv, batch=B, seq_len=L, tq=tm)
        xp = _merge(xp, (o_ret, o_ssm, o_gla, o_fox), row(norm1_g, l), wg_b[l], row(b_mgate, l), wb_b[l], wo_b[l], tm=tm)
        xp, tail = _ffn(xp, row(norm2_g, l), wu_b[l], ffn_conv_w[l], row(ffn_conv_b, l), wd_b[l],
                        prompt=True, batch=B, seq_len=L, tm=tm)
        sr = jnp.transpose(s_ret.reshape(B, 2, HEADS, 32, BR), (0, 2, 1, 3, 4)).reshape(B, BR, BR)
        outs_p.append((_diag_blocks(sr, HEADS, RET_DK, 64), s_re.reshape(B, SSM_G, SSM_P), s_im.reshape(B, SSM_G, SSM_P),
                       _diag_blocks(jnp.transpose(s_gla, (0, 2, 1)), HEADS, GLA_DK, 64),
                       tail[:, SUBLANES - (CONV_W - 1):, :],
                       knat.reshape(B, L, HEADS, FOX_HD), vnat.reshape(B, L, HEADS, FOX_HD),
                       lf[:, :HEADS].reshape(B, L, HEADS)))
        (pret, su, pgla, knat, vnat, lf, fqn) = _inproj(
            xs, row(norm1_g, l), w_in_p[l], wa2[l], row(gla_b_a, l), bfrow[l], cos_s, sin_s, None,
            prompt=False, seq_len=NS, tm=NS)
        (o_ret, o_ssm, o_gla, ret_new, gla_new, re_new, im_new) = _sample_mix(
            pret, pgla, su, state_ret[l], state_gla[l], state_ssm_re[l], state_ssm_im[l],
            ret_norm_g[l], gla_norm_g[l], ab_re[l], ab_im[l], bbcat[l], cccat[l], row(ssm_d, l), wglu_b[l],
            row(ssm_b_glu, l))
        lfs = lf[:, :HEADS]
        o_fox = _fox_sample(page_table, fqn, knat, vnat, lfs, ck3[l], cv3[l], clf3[l])
        xs = _merge(xs, (o_ret.astype(BF16), o_ssm.astype(BF16), o_gla.astype(BF16), o_fox.astype(BF16)),
                    row(norm1_g, l), wg_b[l], row(b_mgate, l), wb_b[l], wo_b[l], tm=NS)
        xs, a_s = _ffn(xs, row(norm2_g, l), wu_b[l], ffn_conv_w[l], row(ffn_conv_b, l), wd_b[l], state_conv[l],
                       prompt=False, batch=NS, seq_len=NS, tm=NS)
        new_conv = jnp.stack([state_conv[l][:, 1, :], a_s], axis=1)
        outs_s.append((ret_new, re_new, im_new, gla_new, new_conv,
                       knat.reshape(NS, 1, HEADS, FOX_HD), vnat.reshape(NS, 1, HEADS, FOX_HD), lfs.reshape(NS, 1, HEADS)))

    y_prompt = _final_norm(xp, final_norm_g[None, :], tm=tm).reshape(B, L, D_MODEL)
    y_sample = _final_norm(xs, final_norm_g[None, :], tm=NS).reshape(NS, 1, D_MODEL)

    def stk(outs, i):
        return jnp.stack([o[i] for o in outs], axis=0)

    return (y_prompt, y_sample,
            stk(outs_p, 5), stk(outs_p, 6), stk(outs_p, 7),
            stk(outs_s, 5), stk(outs_s, 6), stk(outs_s, 7),
            stk(outs_p, 0), stk(outs_s, 0),
            stk(outs_p, 1), stk(outs_p, 2), stk(outs_s, 1), stk(outs_s, 2),
            stk(outs_p, 3), stk(outs_s, 3),
            stk(outs_p, 4), stk(outs_s, 4))
```

```python
import functools
import math

import jax
import jax.numpy as jnp
import numpy as np
from jax import lax
from jax.experimental import pallas as pl
from jax.experimental.pallas import tpu as pltpu

F32 = jnp.float32
BF16 = jnp.bfloat16

LANES = 128
SUBLANES = 8
VMEM_BYTES_V7X = 64 * 1024 * 1024
VMEM_LIMIT = VMEM_BYTES_V7X - 8 * 1024 * 1024

D_MODEL = 1024
BR = 256
N_BRANCH = 4
HEADS = 4
RET_DK = 64
RET_CHUNK = 128
SSM_G = 16
SSM_GC = 16
SSM_P = 64
SSM_N = SSM_G * SSM_P
GLA_DK = 32
GLA_KW = HEADS * GLA_DK
GLA_RANK = 16
GLA_TAU = 16.0
GLA_CHUNK = 32
FOX_HD = 64
D_FF = 2816
FF_CHUNK = 1408
CONV_W = 3
EPS = 1e-6
ROPE_BASE = 10000.0
PAGE = 128
PAGES_PER_STEP = 8
EXP_CLAMP = 80.0
LOG2E = math.log2(math.e)
FOX_QSCALE = (FOX_HD ** -0.5) * LOG2E

C_RQ, C_RK, C_RV, C_RG, C_SU = 0, 256, 512, 768, 1024
C_GQ, C_GK, C_GV, C_GG = 1280, 1408, 1536, 1792
C_FQ, C_FK, C_FV, C_SM = 2048, 2304, 2560, 2816
W_IN_COLS = 2944
G_V, G_G, G_Q, G_K, G_LA = 0, 256, 512, 640, 768
N_RQ, N_RK, N_RV, N_RG, N_SU = 0, 256, 512, 768, 1024
N_GQ, N_GK, N_GV, N_GG, N_GA = 1280, 1408, 1536, 1792, 2048
N_FQ, N_FK, N_FV, N_FF = 2064, 2320, 2576, 2832


def _cparams(sem):
    return pltpu.CompilerParams(dimension_semantics=sem, vmem_limit_bytes=VMEM_LIMIT)


def _dot(a, b):
    return jnp.dot(a, b, preferred_element_type=F32)


def _dot_nt(a, b):
    return lax.dot_general(a, b, (((1,), (1,)), ((), ())), preferred_element_type=F32)


def _dot_tn(a, b):
    return lax.dot_general(a, b, (((0,), (0,)), ((), ())), preferred_element_type=F32)


def _split3(x):
    hi = x.astype(BF16)
    r = x - hi.astype(F32)
    mid = r.astype(BF16)
    lo = (r - mid.astype(F32)).astype(BF16)
    return hi, mid, lo


def _dot3_left(m, x):
    hi, mid, lo = _split3(x)
    return _dot(m, hi) + _dot(m, mid) + _dot(m, lo)


def _dot3_right(x, m):
    hi, mid, lo = _split3(x)
    return _dot(hi, m) + _dot(mid, m) + _dot(lo, m)


def _dot2_right(x, m):
    hi = x.astype(BF16)
    lo = (x - hi.astype(F32)).astype(BF16)
    return _dot(hi, m) + _dot(lo, m)


def _sigmoid(x):
    return 1.0 / (1.0 + jnp.exp(-x))


def _log_sigmoid(x):
    return jnp.minimum(x, 0.0) - jnp.log(1.0 + jnp.exp(-jnp.abs(x)))


def _gelu_tanh(x):
    return 0.5 * x * (1.0 + jnp.tanh(math.sqrt(2.0 / math.pi) * (x + 0.044715 * (x * x * x))))


def _rms(x, g):
    return x * lax.rsqrt(jnp.mean(x * x, axis=-1, keepdims=True) + EPS) * g


def _iota(shape, dim):
    return lax.broadcasted_iota(jnp.int32, shape, dim)


def _head_avg(width, hd):
    r = _iota((width, width), 0) // hd
    c = _iota((width, width), 1) // hd
    return jnp.where(r == c, 1.0 / hd, 0.0).astype(BF16)


def _full(shape):
    nd = len(shape)
    return pl.BlockSpec(shape, lambda *a, _nd=nd: (0,) * _nd)


def _s5_prep_kernel(are_ref, aim_ref, ldt_ref, bre_ref, bim_ref, abre_ref, abim_ref, bbre_ref, bbim_ref):
    a_re = are_ref[...]
    a_im = aim_ref[...]
    dt = jnp.exp(ldt_ref[...])
    mag = jnp.exp(a_re * dt)
    ab_re = mag * jnp.cos(a_im * dt)
    ab_im = mag * jnp.sin(a_im * dt)
    den = a_re * a_re + a_im * a_im
    f_re = ((ab_re - 1.0) * a_re + ab_im * a_im) / den
    f_im = (ab_im * a_re - (ab_re - 1.0) * a_im) / den
    abre_ref[...] = ab_re
    abim_ref[...] = ab_im
    b_re = bre_ref[...]
    b_im = bim_ref[...]
    bbre_ref[...] = f_re[:, None, :] * b_re - f_im[:, None, :] * b_im
    bbim_ref[...] = f_re[:, None, :] * b_im + f_im[:, None, :] * b_re


def _s5_prep(a_re, a_im, log_dt, b_re, b_im):
    depth = a_re.shape[0]
    r = depth * SSM_G
    are = a_re.reshape(r, SSM_P)
    aim = a_im.reshape(r, SSM_P)
    ldt = jnp.broadcast_to(log_dt.reshape(r, 1), (r, SSM_P))
    bre = jnp.transpose(b_re, (0, 1, 3, 2)).reshape(r, SSM_GC, SSM_P)
    bim = jnp.transpose(b_im, (0, 1, 3, 2)).reshape(r, SSM_GC, SSM_P)
    outs = pl.pallas_call(
        _s5_prep_kernel,
        out_shape=(jax.ShapeDtypeStruct((r, SSM_P), F32), jax.ShapeDtypeStruct((r, SSM_P), F32),
                   jax.ShapeDtypeStruct((r, SSM_GC, SSM_P), F32), jax.ShapeDtypeStruct((r, SSM_GC, SSM_P), F32)),
    )(are, aim, ldt, bre, bim)
    ab_re, ab_im, bb_re, bb_im = outs
    ab_re = ab_re.reshape(depth, 1, SSM_N)
    ab_im = ab_im.reshape(depth, 1, SSM_N)
    eye = jnp.eye(SSM_G, dtype=F32)

    def blockdiag(bb):
        bb = bb.reshape(depth, SSM_G, SSM_GC, SSM_P)
        return (bb[:, :, :, None, :] * eye[None, :, None, :, None]).reshape(depth, BR, SSM_N)

    bbcat = jnp.concatenate([blockdiag(bb_re), blockdiag(bb_im)], axis=-1).astype(BF16)
    return ab_re, ab_im, bbcat


def _s5_cmat(c_re, c_im):
    depth = c_re.shape[0]
    eye = jnp.eye(SSM_G, dtype=F32)

    def blockdiag(c):
        ct = jnp.transpose(c, (0, 1, 3, 2))
        return (ct[:, :, :, None, :] * eye[None, :, None, :, None]).reshape(depth, SSM_N, BR)

    return jnp.concatenate([blockdiag(c_re), -blockdiag(c_im)], axis=1).astype(BF16)


def _inproj_kernel(*refs, prompt, tiles_per_batch):
    if prompt:
        (x_ref, g_ref, w_ref, wa2_ref, ba_ref, bf_ref, cos_ref, sin_ref, eall_ref, cq_ref, ck_ref,
         pret_ref, su_ref, pgla_ref, knat_ref, vnat_ref, lf_ref, fq_ref, fk_ref, fv_ref, carry_ref) = refs
    else:
        (x_ref, g_ref, w_ref, wa2_ref, ba_ref, bf_ref, cos_ref, sin_ref,
         pret_ref, su_ref, pgla_ref, knat_ref, vnat_ref, lf_ref, fqn_ref) = refs
    x = x_ref[...]
    tm = x.shape[0]
    h = _rms(x, g_ref[...]).astype(BF16)
    proj = _dot(h, w_ref[...])
    cos = cos_ref[...]
    sin = sin_ref[...]
    for base in (C_RQ, C_RK):
        x1 = proj[:, base:base + 128]
        x2 = proj[:, base + 128:base + 256]
        pret_ref[:, base:base + 128] = x1 * cos - x2 * sin
        pret_ref[:, base + 128:base + 256] = x1 * sin + x2 * cos
    pret_ref[:, C_RV:C_SU] = proj[:, C_RV:C_SU]
    su_ref[...] = proj[:, C_SU:C_GQ]
    small = proj[:, C_SM:C_SM + 128]
    la = _log_sigmoid(_dot(small.astype(BF16), wa2_ref[...]) + ba_ref[...]) * (1.0 / GLA_TAU)
    pgla_ref[:, G_V:G_Q] = proj[:, C_GV:C_FQ]
    pgla_ref[:, G_Q:G_K] = proj[:, C_GQ:C_GK]
    pgla_ref[:, G_K:G_LA] = proj[:, C_GK:C_GV] * (GLA_DK ** -0.5)
    pgla_ref[:, G_LA:G_LA + 128] = la
    lane = _iota((tm, 128), 1)
    lf = jnp.where(lane < HEADS, _log_sigmoid(small + bf_ref[...]), 0.0)
    lf_ref[...] = lf
    knat_ref[...] = proj[:, C_FK:C_FV]
    vnat_ref[...] = proj[:, C_FV:C_SM]
    if not prompt:
        fqn_ref[...] = proj[:, C_FQ:C_FK]
        return
    i = pl.program_id(0)

    @pl.when(i % tiles_per_batch == 0)
    def _():
        carry_ref[...] = jnp.zeros_like(carry_ref)

    tri = (_iota((tm, tm), 0) >= _iota((tm, tm), 1)).astype(BF16)
    c = _dot3_left(tri, lf) + carry_ref[0:1, :]
    carry_ref[...] = jnp.broadcast_to(c[tm - 1:tm, :], carry_ref.shape)
    chi, cmid, clo = _split3(c * LOG2E)
    c3 = (chi.astype(F32) + pltpu.roll(cmid.astype(F32), HEADS, 1)
          + pltpu.roll(clo.astype(F32), 2 * HEADS, 1)).astype(BF16)
    aug = _dot(c3, eall_ref[...])
    cq = cq_ref[...]
    ck = ck_ref[...]
    keep = lane < FOX_HD
    for hd in range(HEADS):
        lo = 128 * (hd // 2)
        bq = proj[:, C_FQ + lo:C_FQ + lo + 128]
        bk = proj[:, C_FK + lo:C_FK + lo + 128]
        bv = proj[:, C_FV + lo:C_FV + lo + 128]
        if hd % 2:
            bq = pltpu.roll(bq, FOX_HD, 1)
            bk = pltpu.roll(bk, FOX_HD, 1)
            bv = pltpu.roll(bv, FOX_HD, 1)
        fq_ref[hd] = (jnp.where(keep, bq, 0.0) + aug[:, 128 * hd:128 * hd + 128] + cq).astype(BF16)
        fk_ref[hd] = (jnp.where(keep, bk, 0.0) + aug[:, 128 * (HEADS + hd):128 * (HEADS + hd) + 128] + ck).astype(BF16)
        fv_ref[hd] = jnp.where(keep, bv, jnp.where(lane == FOX_HD, 1.0, 0.0)).astype(BF16)


def _inproj(x2d, g, w, wa2, ba, bfrow, cos, sin, consts, *, prompt, seq_len, tm):
    t = x2d.shape[0]
    nt = t // tm
    tpb = max(seq_len // tm, 1)
    row = lambda c: pl.BlockSpec((tm, c), lambda i: (i, 0))
    in_specs = [row(D_MODEL), _full((1, D_MODEL)), _full((D_MODEL, W_IN_COLS)), _full((128, 128)),
                _full((1, 128)), _full((1, 128)),
                pl.BlockSpec((tm, 128), lambda i: (i % tpb, 0)), pl.BlockSpec((tm, 128), lambda i: (i % tpb, 0))]
    args = [x2d, g, w, wa2, ba, bfrow, cos, sin]
    out_shape = [jax.ShapeDtypeStruct((t, 1024), F32), jax.ShapeDtypeStruct((t, BR), F32),
                 jax.ShapeDtypeStruct((t, 896), F32), jax.ShapeDtypeStruct((t, BR), F32),
                 jax.ShapeDtypeStruct((t, BR), F32), jax.ShapeDtypeStruct((t, 128), F32)]
    out_specs = [row(1024), row(BR), row(896), row(BR), row(BR), row(128)]
    scratch = []
    if prompt:
        eall, cq, ck = consts
        in_specs += [_full((128, 8 * 128)), _full((1, 128)), _full((1, 128))]
        args += [eall, cq, ck]
        hm = pl.BlockSpec((HEADS, tm, 128), lambda i: (0, i, 0))
        out_shape += [jax.ShapeDtypeStruct((HEADS, t, 128), BF16)] * 3
        out_specs += [hm, hm, hm]
        scratch = [pltpu.VMEM((SUBLANES, 128), F32)]
    else:
        out_shape += [jax.ShapeDtypeStruct((t, BR), F32)]
        out_specs += [row(BR)]
    return pl.pallas_call(
        functools.partial(_inproj_kernel, prompt=prompt, tiles_per_batch=tpb),
        grid=(nt,), in_specs=in_specs, out_specs=out_specs, out_shape=out_shape,
        scratch_shapes=scratch, compiler_params=_cparams(("arbitrary",)),
        name="inproj_prompt" if prompt else "inproj_sample",
    )(*args)


def _ret_log_gamma():
    return [math.log(1.0 - 2.0 ** (-5.0 - h)) for h in range(HEADS)]


def _lane_select(idx, values):
    out = jnp.zeros(idx.shape, F32)
    for h, v in enumerate(values):
        out = jnp.where(idx == h, v, out)
    return out


def _ret_kernel(q_ref, k_ref, v_ref, g_ref, gn_ref, o_ref, s_ref, S, *, nchunk):
    j = pl.program_id(1)
    c = RET_CHUNK

    @pl.when(j == 0)
    def _():
        S[...] = jnp.zeros_like(S)

    lg = _ret_log_gamma()
    lane = _iota((1, BR), 1)
    head_qk = (lane & 127) >> 5
    head_v = lane >> 6
    lg_lane = _lane_select(head_qk, lg)
    t = _iota((c, 1), 0).astype(F32)
    gq = jnp.exp((t + 1.0) * lg_lane)
    gk = jnp.exp((c - 1.0 - t) * lg_lane)
    rowi = _iota((BR, 1), 0)
    head_row = (rowi & 127) >> 5
    gam = jnp.exp(float(c) * _lane_select(head_row, lg))
    bd = head_row == head_v
    dt = (_iota((c, c), 0) - _iota((c, c), 1))
    dmat = jnp.concatenate(
        [jnp.where(dt >= 0, jnp.exp(dt.astype(F32) * lg[h]), 0.0) for h in range(HEADS)], axis=0)
    avg = _head_avg(BR, RET_DK)
    gn = gn_ref[...]
    for ci in range(nchunk):
        rows = slice(ci * c, (ci + 1) * c)
        q = q_ref[rows, :]
        k = k_ref[rows, :]
        vb = v_ref[rows, :].astype(BF16)
        qs = jnp.concatenate([jnp.where(head_qk == h, q, 0.0) for h in range(HEADS)], axis=0).astype(BF16)
        sc = (_dot_nt(qs, k.astype(BF16)) * dmat).astype(BF16)
        pv = _dot(sc, vb)
        o = _dot((q * gq).astype(BF16), S[...].astype(BF16))
        for h in range(HEADS):
            o = o + jnp.where(head_v == h, pv[h * c:(h + 1) * c, :], 0.0)
        S[...] = gam * S[...] + jnp.where(bd, _dot_tn((k * gk).astype(BF16), vb), 0.0)
        oc = o - _dot2_right(o, avg)
        on = oc * lax.rsqrt(_dot2_right(oc * oc, avg) + EPS)
        gt = g_ref[rows, :]
        o_ref[rows, :] = (on * gn * (gt * _sigmoid(gt))).astype(BF16)

    @pl.when(j == pl.num_programs(1) - 1)
    def _():
        s_ref[0] = S[...]


def _retention(pret, gn, *, batch, seq_len, tc):
    t = pret.shape[0]
    nj = seq_len // tc
    col = lambda cb: pl.BlockSpec((tc, BR), lambda b, j, _cb=cb: (b * nj + j, _cb))
    return pl.pallas_call(
        functools.partial(_ret_kernel, nchunk=tc // RET_CHUNK), name="retention",
        grid=(batch, nj),
        in_specs=[col(0), col(1), col(2), col(3), _full((1, BR))],
        out_specs=[pl.BlockSpec((tc, BR), lambda b, j: (b * nj + j, 0)),
                   pl.BlockSpec((1, BR, BR), lambda b, j: (b, 0, 0))],
        out_shape=[jax.ShapeDtypeStruct((t, BR), BF16), jax.ShapeDtypeStruct((batch, BR, BR), F32)],
        scratch_shapes=[pltpu.VMEM((BR, BR), F32)],
        compiler_params=_cparams(("arbitrary", "arbitrary")),
    )(pret, pret, pret, pret, gn)


def _cmul(ar, ai, br, bi):
    return ar * br - ai * bi, ar * bi + ai * br


def _s5_kernel(u_ref, abre_ref, abim_ref, bb_ref, cc_ref, d_ref, wg_ref, bg_ref,
               y_ref, sre_ref, sim_ref, xr_ref, xi_ref, car_ref):
    j = pl.program_id(1)
    tm = u_ref.shape[0]

    @pl.when(j == 0)
    def _():
        car_ref[...] = jnp.zeros_like(car_ref)

    a1r = abre_ref[...]
    a1i = abim_ref[...]
    u = u_ref[...]
    bu = _dot(u.astype(BF16), bb_ref[...])
    xr = bu[:, :SSM_N]
    xi = bu[:, SSM_N:]
    row8 = _iota((tm, 1), 0) & 7
    pr, pi_ = a1r, a1i
    for k in (1, 2, 4):
        ok = row8 >= k
        sr = jnp.where(ok, pltpu.roll(xr, k, 0), 0.0)
        si = jnp.where(ok, pltpu.roll(xi, k, 0), 0.0)
        dr, di = _cmul(pr, pi_, sr, si)
        xr = xr + dr
        xi = xi + di
        pr, pi_ = _cmul(pr, pi_, pr, pi_)
    r8 = _iota((SUBLANES, 1), 0)
    tr = jnp.broadcast_to(a1r, (SUBLANES, SSM_N))
    ti = jnp.broadcast_to(a1i, (SUBLANES, SSM_N))
    qr, qi = a1r, a1i
    for bit in range(3):
        nr, ni = _cmul(tr, ti, qr, qi)
        use = ((r8 >> bit) & 1) == 1
        tr = jnp.where(use, nr, tr)
        ti = jnp.where(use, ni, ti)
        qr, qi = _cmul(qr, qi, qr, qi)
    xr_ref[...] = xr
    xi_ref[...] = xi

    def body(g, carry):
        cr, ci = carry
        rows = pl.ds(pl.multiple_of(g * SUBLANES, SUBLANES), SUBLANES)
        dr, di = _cmul(tr, ti, cr, ci)
        r2 = xr_ref[rows, :] + dr
        i2 = xi_ref[rows, :] + di
        xr_ref[rows, :] = r2
        xi_ref[rows, :] = i2
        return r2[SUBLANES - 1:SUBLANES, :], i2[SUBLANES - 1:SUBLANES, :]

    cr, ci = lax.fori_loop(0, tm // SUBLANES, body, (car_ref[0:1, :], car_ref[1:2, :]))
    car_ref[0:1, :] = cr
    car_ref[1:2, :] = ci
    sre_ref[0] = cr
    sim_ref[0] = ci
    xcat = jnp.concatenate([xr_ref[...].astype(BF16), xi_ref[...].astype(BF16)], axis=1)
    y = _dot(xcat, cc_ref[...]) + d_ref[...] * u
    yg = _gelu_tanh(y)
    y_ref[...] = (yg * _sigmoid(_dot(yg.astype(BF16), wg_ref[...]) + bg_ref[...])).astype(BF16)


def _s5(su, ab_re, ab_im, bbcat, cccat, d_skip, w_glu, b_glu, *, batch, seq_len, tm):
    t = su.shape[0]
    nj = seq_len // tm
    return pl.pallas_call(
        _s5_kernel, name="s5_scan",
        grid=(batch, nj),
        in_specs=[pl.BlockSpec((tm, BR), lambda b, j: (b * nj + j, 0)), _full((1, SSM_N)), _full((1, SSM_N)),
                  _full((BR, 2 * SSM_N)), _full((2 * SSM_N, BR)), _full((1, BR)), _full((BR, BR)), _full((1, BR))],
        out_specs=[pl.BlockSpec((tm, BR), lambda b, j: (b * nj + j, 0)),
                   pl.BlockSpec((1, 1, SSM_N), lambda b, j: (b, 0, 0)),
                   pl.BlockSpec((1, 1, SSM_N), lambda b, j: (b, 0, 0))],
        out_shape=[jax.ShapeDtypeStruct((t, BR), BF16), jax.ShapeDtypeStruct((batch, 1, SSM_N), F32),
                   jax.ShapeDtypeStruct((batch, 1, SSM_N), F32)],
        scratch_shapes=[pltpu.VMEM((tm, SSM_N), F32), pltpu.VMEM((tm, SSM_N), F32), pltpu.VMEM((SUBLANES, SSM_N), F32)],
        compiler_params=_cparams(("arbitrary", "arbitrary")),
    )(su, ab_re, ab_im, bbcat, cccat, d_skip, w_glu, b_glu)


def _gla_kernel(q_ref, k_ref, la_ref, v_ref, g_ref, gn_ref, o_ref, s_ref, ST, oi_ref, qt_ref, kh_ref, el_ref):
    j = pl.program_id(1)
    tc = q_ref.shape[0]
    c = GLA_CHUNK
    nchunk = tc // c

    @pl.when(j == 0)
    def _():
        ST[...] = jnp.zeros_like(ST)

    r = _iota((tc, tc), 0)
    s = _iota((tc, tc), 1)
    same = (r // c) == (s // c)
    tri = (same & (s <= r)).astype(BF16)
    ones_blk = same.astype(BF16)
    la = la_ref[...]
    hi, mid, lo = _split3(la)
    b = _dot(tri, hi) + _dot(tri, mid) + _dot(tri, lo)
    bl = _dot(ones_blk, hi) + _dot(ones_blk, mid) + _dot(ones_blk, lo)
    q = q_ref[...]
    k = k_ref[...]
    qt = q * jnp.exp(b)
    kt = k * jnp.exp(jnp.minimum(-b, EXP_CLAMP))
    qt_ref[...] = qt
    kh_ref[...] = k * jnp.exp(bl - b)
    el_ref[...] = jnp.exp(bl)
    lane_k = _iota((1, GLA_KW), 1) >> 5
    lane_v = _iota((1, BR), 1) >> 6
    qs = jnp.concatenate([jnp.where(lane_k == h, qt, 0.0) for h in range(HEADS)], axis=0).astype(BF16)
    sc = _dot_nt(qs, kt.astype(BF16))
    keep = jnp.concatenate([same & (s <= r)] * HEADS, axis=0)
    sc = jnp.where(keep, sc, 0.0).astype(BF16)
    vb = v_ref[...].astype(BF16)
    pv = _dot(sc, vb)
    o = jnp.zeros((tc, BR), F32)
    for h in range(HEADS):
        o = o + jnp.where(lane_v == h, pv[h * tc:(h + 1) * tc, :], 0.0)
    oi_ref[...] = o
    bd = (_iota((BR, 1), 0) >> 6) == lane_k

    def body(ci, carry):
        rows = pl.ds(pl.multiple_of(ci * c, c), c)
        st = ST[...]
        oi_ref[rows, :] = oi_ref[rows, :] + _dot_nt(qt_ref[rows, :].astype(BF16), st.astype(BF16))
        upd = _dot_tn(v_ref[rows, :].astype(BF16), kh_ref[rows, :].astype(BF16))
        last = pl.ds(pl.multiple_of(ci * c, c) + (c - 1), 1)
        ST[...] = el_ref[last, :] * st + jnp.where(bd, upd, 0.0)
        return carry

    lax.fori_loop(0, nchunk, body, 0)
    o = oi_ref[...]
    avg = _head_avg(BR, BR // HEADS)
    on = o * lax.rsqrt(_dot2_right(o * o, avg) + EPS)
    gt = g_ref[...]
    o_ref[...] = (on * gn_ref[...] * (gt * _sigmoid(gt))).astype(BF16)

    @pl.when(j == pl.num_programs(1) - 1)
    def _():
        s_ref[0] = ST[...]


def _gla(pgla, gn, *, batch, seq_len, tc):
    t = pgla.shape[0]
    nj = seq_len // tc
    c128 = lambda cb: pl.BlockSpec((tc, 128), lambda b, j, _cb=cb: (b * nj + j, _cb))
    c256 = lambda cb: pl.BlockSpec((tc, BR), lambda b, j, _cb=cb: (b * nj + j, _cb))
    return pl.pallas_call(
        _gla_kernel, name="gla",
        grid=(batch, nj),
        in_specs=[c128(G_Q // 128), c128(G_K // 128), c128(G_LA // 128), c256(G_V // BR), c256(G_G // BR),
                  _full((1, BR))],
        out_specs=[pl.BlockSpec((tc, BR), lambda b, j: (b * nj + j, 0)),
                   pl.BlockSpec((1, BR, GLA_KW), lambda b, j: (b, 0, 0))],
        out_shape=[jax.ShapeDtypeStruct((t, BR), BF16), jax.ShapeDtypeStruct((batch, BR, GLA_KW), F32)],
        scratch_shapes=[pltpu.VMEM((BR, GLA_KW), F32), pltpu.VMEM((tc, BR), F32), pltpu.VMEM((tc, GLA_KW), F32),
                        pltpu.VMEM((tc, GLA_KW), F32), pltpu.VMEM((tc, GLA_KW), F32)],
        compiler_params=_cparams(("arbitrary", "arbitrary")),
    )(pgla, pgla, pgla, pgla, pgla, gn)


NEG = -1e30


def _fox_kernel(qi_tab, kj_tab, q_ref, k_ref, v_ref, o_ref, m_ref, acc_ref):
    p = pl.program_id(1)
    qi = qi_tab[p]
    kj = kj_tab[p]
    tq = q_ref.shape[1]
    ntile = tq // 128

    @pl.when(kj == 0)
    def _():
        m_ref[...] = jnp.full_like(m_ref, NEG)
        acc_ref[...] = jnp.zeros_like(acc_ref)

    def step(diag):
        if diag:
            causal = _iota((tq, tq), 1) <= _iota((tq, tq), 0)
        for h in range(HEADS):
            s = _dot_nt(q_ref[h], k_ref[h])
            if diag:
                s = jnp.where(causal, s, NEG)
            tiles = [s[:, 128 * c:128 * (c + 1)] for c in range(ntile)]
            m4 = functools.reduce(jnp.maximum, tiles)
            m_prev = m_ref[h]
            m_new = jnp.maximum(m_prev, jnp.max(m4, axis=1, keepdims=True))
            alpha = jnp.exp2(m_prev - m_new)
            pe = jnp.concatenate([jnp.exp2(t - m_new).astype(BF16) for t in tiles], axis=1)
            acc_ref[h] = alpha * acc_ref[h] + _dot(pe, v_ref[h])
            m_ref[h] = m_new

    @pl.when(kj != qi)
    def _():
        step(False)

    @pl.when(kj == qi)
    def _():
        step(True)
        low = _iota((tq, 128), 1) < FOX_HD
        for pair in range(HEADS // 2):
            a0 = acc_ref[2 * pair]
            a1 = acc_ref[2 * pair + 1]
            oe = jnp.where(low, a0 / a0[:, FOX_HD:FOX_HD + 1], 0.0)
            oo = jnp.where(low, a1 / a1[:, FOX_HD:FOX_HD + 1], 0.0)
            o_ref[:, 128 * pair:128 * pair + 128] = (oe + pltpu.roll(oo, FOX_HD, 1)).astype(BF16)


def _fox_prompt(fq, fk, fv, *, batch, seq_len, tq):
    t = fq.shape[1]
    nq = seq_len // tq
    pairs = [(i, j) for i in range(nq) for j in range(i + 1)]
    qi_tab = jnp.asarray([p[0] for p in pairs], jnp.int32)
    kj_tab = jnp.asarray([p[1] for p in pairs], jnp.int32)
    grid_spec = pltpu.PrefetchScalarGridSpec(
        num_scalar_prefetch=2, grid=(batch, len(pairs)),
        in_specs=[pl.BlockSpec((HEADS, tq, 128), lambda b, p, qt, kt: (0, b * nq + qt[p], 0)),
                  pl.BlockSpec((HEADS, tq, 128), lambda b, p, qt, kt: (0, b * nq + kt[p], 0)),
                  pl.BlockSpec((HEADS, tq, 128), lambda b, p, qt, kt: (0, b * nq + kt[p], 0))],
        out_specs=pl.BlockSpec((tq, BR), lambda b, p, qt, kt: (b * nq + qt[p], 0)),
        scratch_shapes=[pltpu.VMEM((HEADS, tq, 128), F32)] * 2)
    return pl.pallas_call(
        _fox_kernel, grid_spec=grid_spec, out_shape=jax.ShapeDtypeStruct((t, BR), BF16), name="fox_prompt",
        compiler_params=_cparams(("arbitrary", "arbitrary")),
    )(qi_tab, kj_tab, fq, fk, fv)


def _merge_kernel(x_ref, o0_ref, o1_ref, o2_ref, o3_ref, g_ref, wg_ref, bg_ref, wb_ref, wo_ref, out_ref):
    x = x_ref[...]
    h = _rms(x, g_ref[...]).astype(BF16)
    merged = None
    for n, o_ref in enumerate((o0_ref, o1_ref, o2_ref, o3_ref)):
        cols = slice(n * D_MODEL, (n + 1) * D_MODEL)
        gate = _sigmoid(_dot(h, wg_ref[:, cols]) + bg_ref[:, cols])
        term = gate * _dot(o_ref[...], wb_ref[n])
        merged = term if merged is None else merged + term
    out_ref[...] = x + _dot(merged.astype(BF16), wo_ref[...])


def _merge(x2d, branches, g, wg, bg, wb, wo, *, tm):
    t = x2d.shape[0]
    row = lambda c: pl.BlockSpec((tm, c), lambda i: (i, 0))
    return pl.pallas_call(
        _merge_kernel, grid=(t // tm,), name="merge",
        in_specs=[row(D_MODEL), row(BR), row(BR), row(BR), row(BR), _full((1, D_MODEL)),
                  _full((D_MODEL, N_BRANCH * D_MODEL)), _full((1, N_BRANCH * D_MODEL)),
                  _full((N_BRANCH, BR, D_MODEL)), _full((D_MODEL, D_MODEL))],
        out_specs=row(D_MODEL), out_shape=jax.ShapeDtypeStruct((t, D_MODEL), F32),
        compiler_params=_cparams(("arbitrary",)),
    )(x2d, *branches, g, wg, bg, wb, wo)


def _ffn_kernel(*refs, prompt, tiles_per_batch):
    if prompt:
        x_ref, g_ref, wu_ref, cw_ref, cb_ref, wd_ref, out_ref, tail_ref, prev_ref = refs
    else:
        x_ref, g_ref, wu_ref, cw_ref, cb_ref, wd_ref, c0_ref, c1_ref, out_ref, a_ref = refs
    x = x_ref[...]
    tm = x.shape[0]
    h = _rms(x, g_ref[...]).astype(BF16)
    if prompt:
        i = pl.program_id(0)

        @pl.when(i % tiles_per_batch == 0)
        def _():
            prev_ref[...] = jnp.zeros_like(prev_ref)

        rowid = _iota((tm, 1), 0)
    y = x
    for ch in range(D_FF // FF_CHUNK):
        cols = slice(ch * FF_CHUNK, (ch + 1) * FF_CHUNK)
        gcols = slice(D_FF + ch * FF_CHUNK, D_FF + (ch + 1) * FF_CHUNK)
        a = _dot(h, wu_ref[:, cols])
        gate = _dot(h, wu_ref[:, gcols])
        if prompt:
            p1 = prev_ref[SUBLANES - 1:SUBLANES, cols]
            p2 = prev_ref[SUBLANES - 2:SUBLANES - 1, cols]
            a1 = jnp.where(rowid == 0, p1, pltpu.roll(a, 1, 0))
            a2 = jnp.where(rowid == 0, p2, jnp.where(rowid == 1, p1, pltpu.roll(a, 2, 0)))
            prev_ref[:, cols] = a[tm - SUBLANES:tm, :]
            tail_ref[0, :, cols] = a[tm - SUBLANES:tm, :]
        else:
            a1 = c1_ref[:, cols]
            a2 = c0_ref[:, cols]
            a_ref[:, cols] = a
        conv = cb_ref[:, cols] + cw_ref[0:1, cols] * a2 + cw_ref[1:2, cols] * a1 + cw_ref[2:3, cols] * a
        act = (_gelu_tanh(conv) * gate).astype(BF16)
        y = y + _dot(act, wd_ref[cols, :])
    out_ref[...] = y


def _ffn(x2d, g, wu, cw, cb, wd, conv0=None, *, prompt, batch, seq_len, tm):
    t = x2d.shape[0]
    tpb = max(seq_len // tm, 1)
    row = lambda c: pl.BlockSpec((tm, c), lambda i: (i, 0))
    in_specs = [row(D_MODEL), _full((1, D_MODEL)), _full((D_MODEL, 2 * D_FF)), _full((CONV_W, D_FF)),
                _full((1, D_FF)), _full((D_FF, D_MODEL))]
    args = [x2d, g, wu, cw, cb, wd]
    if prompt:
        out_shape = [jax.ShapeDtypeStruct((t, D_MODEL), F32), jax.ShapeDtypeStruct((batch, SUBLANES, D_FF), F32)]
        out_specs = [row(D_MODEL), pl.BlockSpec((1, SUBLANES, D_FF), lambda i: (i // tpb, 0, 0))]
        scratch = [pltpu.VMEM((SUBLANES, D_FF), F32)]
    else:
        in_specs += [row(D_FF), row(D_FF)]
        args += [conv0[:, 0, :], conv0[:, 1, :]]
        out_shape = [jax.ShapeDtypeStruct((t, D_MODEL), F32), jax.ShapeDtypeStruct((t, D_FF), F32)]
        out_specs = [row(D_MODEL), row(D_FF)]
        scratch = []
    return pl.pallas_call(
        functools.partial(_ffn_kernel, prompt=prompt, tiles_per_batch=tpb), name="conv_ffn",
        grid=(t // tm,), in_specs=in_specs, out_specs=out_specs, out_shape=out_shape,
        scratch_shapes=scratch, compiler_params=_cparams(("arbitrary",)),
    )(*args)


def _norm_kernel(x_ref, g_ref, o_ref):
    o_ref[...] = _rms(x_ref[...], g_ref[...])


def _final_norm(x2d, g, *, tm):
    t = x2d.shape[0]
    row = pl.BlockSpec((tm, D_MODEL), lambda i: (i, 0))
    return pl.pallas_call(
        _norm_kernel, grid=(t // tm,), in_specs=[row, _full((1, D_MODEL))], out_specs=row,
        out_shape=jax.ShapeDtypeStruct((t, D_MODEL), F32), compiler_params=_cparams(("arbitrary",)),
    )(x2d, g)


def _sample_mix_kernel(rq_ref, rk_ref, rv_ref, rg_ref, rs_ref, rgn_ref,
                       gq_ref, gk_ref, gla_ref, gv_ref, gg_ref, gs_ref, ggn_ref,
                       u_ref, sre_ref, sim_ref, abre_ref, abim_ref, bb_ref, cc_ref, d_ref, wg_ref, bg_ref,
                       oret_ref, rsn_ref, ogla_ref, gsn_ref, ossm_ref, sren_ref, simn_ref):
    lg = _ret_log_gamma()
    v = rv_ref[...]
    for h in range(HEADS):
        sn = math.exp(lg[h]) * rs_ref[:, h] + rk_ref[:, h] * v[:, h]
        rsn_ref[:, h] = sn
        o = jnp.sum(rq_ref[:, h] * sn, axis=1)
        oc = o - jnp.mean(o, axis=-1, keepdims=True)
        on = oc * lax.rsqrt(jnp.mean(oc * oc, axis=-1, keepdims=True) + EPS)
        gt = rg_ref[:, h]
        oret_ref[:, h] = on * rgn_ref[h] * (gt * _sigmoid(gt))
    gv = gv_ref[...]
    for h in range(HEADS):
        sn = jnp.exp(gla_ref[:, h]) * gs_ref[:, h] + gk_ref[:, h] * gv[:, h]
        gsn_ref[:, h] = sn
        o = jnp.sum(gq_ref[:, h] * sn, axis=1)
        on = o * lax.rsqrt(jnp.mean(o * o, axis=-1, keepdims=True) + EPS)
        gt = gg_ref[:, h]
        ogla_ref[:, h] = on * ggn_ref[h] * (gt * _sigmoid(gt))
    u = u_ref[...]
    bu = _dot(u.astype(BF16), bb_ref[...])
    dr, di = _cmul(abre_ref[...], abim_ref[...], sre_ref[...], sim_ref[...])
    xr = bu[:, :SSM_N] + dr
    xi = bu[:, SSM_N:] + di
    sren_ref[...] = xr
    simn_ref[...] = xi
    xcat = jnp.concatenate([xr.astype(BF16), xi.astype(BF16)], axis=1)
    y = _dot(xcat, cc_ref[...]) + d_ref[...] * u
    yg = _gelu_tanh(y)
    ossm_ref[...] = yg * _sigmoid(_dot(yg.astype(BF16), wg_ref[...]) + bg_ref[...])


def _sample_mix(pret, pgla, su, st_ret, st_gla, st_re, st_im, rgn, ggn, ab_re, ab_im, bbcat, cccat, d_skip, w_glu, b_glu):
    n = pret.shape[0]

    def unpack_qk(a):
        return jnp.transpose(a.reshape(n, 2, HEADS, 32), (0, 2, 1, 3)).reshape(n, HEADS, RET_DK, 1)

    rq = unpack_qk(pret[:, 0:256])
    rk = unpack_qk(pret[:, 256:512])
    rv = pret[:, 512:768].reshape(n, HEADS, 1, 64)
    rg = pret[:, 768:1024].reshape(n, HEADS, 64)
    gq = pgla[:, G_Q:G_K].reshape(n, HEADS, GLA_DK, 1)
    gk = pgla[:, G_K:G_LA].reshape(n, HEADS, GLA_DK, 1)
    gla = pgla[:, G_LA:G_LA + 128].reshape(n, HEADS, GLA_DK, 1)
    gv = pgla[:, G_V:G_G].reshape(n, HEADS, 1, 64)
    gg = pgla[:, G_G:G_Q].reshape(n, HEADS, 64)
    out_shape = [jax.ShapeDtypeStruct((n, HEADS, 64), F32), jax.ShapeDtypeStruct(st_ret.shape, F32),
                 jax.ShapeDtypeStruct((n, HEADS, 64), F32), jax.ShapeDtypeStruct(st_gla.shape, F32),
                 jax.ShapeDtypeStruct((n, BR), F32), jax.ShapeDtypeStruct((n, SSM_N), F32),
                 jax.ShapeDtypeStruct((n, SSM_N), F32)]
    outs = pl.pallas_call(
        _sample_mix_kernel, out_shape=out_shape, name="sample_mixers",
        compiler_params=pltpu.CompilerParams(vmem_limit_bytes=VMEM_LIMIT),
    )(rq, rk, rv, rg, st_ret, rgn.reshape(HEADS, 1, 64), gq, gk, gla, gv, gg, st_gla, ggn.reshape(HEADS, 1, 64),
      su, st_re.reshape(n, SSM_N), st_im.reshape(n, SSM_N), ab_re, ab_im, bbcat, cccat, d_skip, w_glu, b_glu)
    o_ret, ret_new, o_gla, gla_new, o_ssm, re_new, im_new = outs
    return (o_ret.reshape(n, BR), o_ssm, o_gla.reshape(n, BR), ret_new, gla_new,
            re_new.reshape(n, SSM_G, SSM_P), im_new.reshape(n, SSM_G, SSM_P))


def _fox_sample_kernel(*refs, npp):
    pt_ref = refs[0]
    q_ref, kn_ref, vn_ref, lfn_ref = refs[1:5]
    k_refs = refs[5:5 + npp]
    v_refs = refs[5 + npp:5 + 2 * npp]
    lf_refs = refs[5 + 2 * npp:5 + 3 * npp]
    o_ref, qb_ref, m_ref, l_ref, acc_ref, car_ref = refs[5 + 3 * npp:]
    p = pl.program_id(1)
    lane4 = _iota((HEADS, PAGE), 1)
    row4 = _iota((HEADS, PAGE), 0)

    def head_rows(x):
        out = jnp.zeros((HEADS, x.shape[1]), F32)
        for h in range(HEADS):
            r = jnp.sum(x[FOX_HD * h:FOX_HD * (h + 1), :], axis=0, keepdims=True)
            out = jnp.where(_iota(out.shape, 0) == h, r, out)
        return out

    @pl.when(p == 0)
    def _():
        qcol = q_ref[0]
        qb_ref[...] = jnp.broadcast_to(qcol, qb_ref.shape)
        m_ref[...] = jnp.broadcast_to(head_rows(qcol * kn_ref[0]), m_ref.shape)
        l_ref[...] = jnp.ones_like(l_ref)
        acc_ref[...] = jnp.where(_iota(acc_ref.shape, 1) == 0, vn_ref[0], 0.0)
        car_ref[...] = jnp.broadcast_to(lfn_ref[0], car_ref.shape)

    qb = qb_ref[...]
    car = car_ref[...]
    logits = []
    for i in range(npp):
        lf = lf_refs[i][0, 0]
        suf = lf
        for k in (1, 2, 4, 8, 16, 32, 64):
            suf = suf + jnp.where(lane4 < PAGE - k, pltpu.roll(suf, PAGE - k, 1), 0.0)
        s = head_rows(k_refs[i][0, 0] * qb) + ((suf - lf) + car) * LOG2E
        car = car + suf[:, 0:1]
        logits.append(s)
    m_prev = m_ref[...]
    m_new = jnp.maximum(m_prev, jnp.max(functools.reduce(jnp.maximum, logits), axis=1, keepdims=True))
    alpha = jnp.exp2(m_prev - m_new)
    probs = [jnp.exp2(s - m_new) for s in logits]
    l_ref[...] = alpha * l_ref[...] + jnp.sum(functools.reduce(jnp.add, probs), axis=1, keepdims=True)
    m_ref[...] = m_new
    car_ref[...] = car
    for h in range(HEADS):
        rows = slice(FOX_HD * h, FOX_HD * (h + 1))
        a = alpha[h:h + 1, :] * acc_ref[rows, :]
        for i in range(npp):
            a = a + v_refs[i][0, 0, rows, :] * probs[i][h:h + 1, :]
        acc_ref[rows, :] = a

    @pl.when(p == pl.num_programs(1) - 1)
    def _():
        for h in range(HEADS):
            rows = slice(FOX_HD * h, FOX_HD * (h + 1))
            o_ref[0, rows, :] = jnp.sum(acc_ref[rows, :], axis=1, keepdims=True) / l_ref[h:h + 1, 0:1]


def _fox_sample(page_table, fq, kn, vn, lfn, cache_kt, cache_vt, cache_lft, layer):
    n, n_pages = page_table.shape
    npp = PAGES_PER_STEP
    while n_pages % npp:
        npp //= 2
    steps = n_pages // npp
    pt = page_table.reshape(-1)

    def page_spec(shape, i):
        return pl.BlockSpec(
            shape, lambda b, p, pt_, _i=i: (layer, pt_[b * n_pages + n_pages - 1 - (p * npp + _i)], 0, 0))

    col3 = lambda r: pl.BlockSpec((1, r, 1), lambda b, p, pt_: (b, 0, 0))
    in_specs = ([col3(BR), col3(BR), col3(BR), col3(HEADS)]
                + [page_spec((1, 1, BR, PAGE), i) for i in range(npp)]
                + [page_spec((1, 1, BR, PAGE), i) for i in range(npp)]
                + [page_spec((1, 1, HEADS, PAGE), i) for i in range(npp)])
    grid_spec = pltpu.PrefetchScalarGridSpec(
        num_scalar_prefetch=1, grid=(n, steps), in_specs=in_specs, out_specs=col3(BR),
        scratch_shapes=[pltpu.VMEM((BR, PAGE), F32), pltpu.VMEM((HEADS, PAGE), F32), pltpu.VMEM((HEADS, PAGE), F32),
                        pltpu.VMEM((BR, PAGE), F32), pltpu.VMEM((HEADS, PAGE), F32)])
    out = pl.pallas_call(
        functools.partial(_fox_sample_kernel, npp=npp), grid_spec=grid_spec,
        out_shape=jax.ShapeDtypeStruct((n, BR, 1), F32), name="fox_sample",
        compiler_params=_cparams(("arbitrary", "arbitrary")),
    )(pt, fq[:, :, None], kn[:, :, None], vn[:, :, None], lfn[:, :, None],
      *([cache_kt] * npp), *([cache_vt] * npp), *([cache_lft] * npp))
    return out.reshape(n, BR)


def _pack_w_in(w):
    def rope_pack(blk):
        d = blk.shape[0]
        b4 = blk.reshape(d, D_MODEL, HEADS, 2, 32)
        return jnp.transpose(b4, (0, 1, 3, 2, 4)).reshape(d, D_MODEL, 256)

    depth = w.shape[0]
    small = jnp.zeros((depth, D_MODEL, 128), w.dtype)
    small = small.at[:, :, 0:HEADS].set(w[:, :, N_FF:N_FF + HEADS])
    small = small.at[:, :, HEADS:HEADS + GLA_RANK].set(w[:, :, N_GA:N_GA + GLA_RANK])
    parts = [rope_pack(w[:, :, N_RQ:N_RK]), rope_pack(w[:, :, N_RK:N_RV]) * (RET_DK ** -0.5),
             w[:, :, N_RV:N_GQ],
             w[:, :, N_GQ:N_GA],
             w[:, :, N_FQ:N_FK] * FOX_QSCALE, w[:, :, N_FK:N_FF], small]
    return jnp.concatenate(parts, axis=-1).astype(BF16)


def _rope_tables(pos):
    inv = ROPE_BASE ** (-jnp.arange(32, dtype=F32) * 2.0 / RET_DK)
    ang = pos.astype(F32)[:, None] * inv[None, :]
    return jnp.tile(jnp.cos(ang), (1, HEADS)), jnp.tile(jnp.sin(ang), (1, HEADS))


def _fox_consts():
    eall = np.zeros((128, 8 * 128), np.float32)
    for h in range(HEADS):
        for part in range(3):
            eall[part * HEADS + h, 128 * h + FOX_HD + part] = 1.0
            eall[part * HEADS + h, 128 * (HEADS + h) + FOX_HD + 3 + part] = -1.0
    cq = np.zeros((1, 128), np.float32)
    ck = np.zeros((1, 128), np.float32)
    cq[0, FOX_HD + 3:FOX_HD + 6] = 1.0
    ck[0, FOX_HD:FOX_HD + 3] = 1.0
    return jnp.asarray(eall, BF16), jnp.asarray(cq), jnp.asarray(ck)


def _diag_blocks(s, nh, dk, dv):
    b = s.shape[0]
    s5 = s.reshape(b, nh, dk, nh, dv)
    idx = jnp.arange(nh)
    return jnp.transpose(s5[:, idx, :, idx, :], (1, 0, 2, 3))


def kernel(x_prompt, x_sample, cache_k, cache_v, cache_logf, state_ret, state_ssm_re, state_ssm_im, state_gla, state_conv, page_table, norm1_g, w_in, w_mgate, b_mgate, ret_norm_g, ssm_a_re, ssm_a_im, ssm_log_dt, ssm_b_re, ssm_b_im, ssm_c_re, ssm_c_im, ssm_d, ssm_w_glu, ssm_b_glu, gla_w_a2, gla_b_a, gla_norm_g, fox_b_f, w_branch, w_out, norm2_g, ffn_w_up, ffn_conv_w, ffn_conv_b, ffn_w_down, final_norm_g):
    depth = w_in.shape[0]
    B, L, _ = x_prompt.shape
    NS = x_sample.shape[0]
    n_pages = page_table.shape[1]
    past = n_pages * PAGE
    T = B * L
    tm = min(512, L)
    n_pool = cache_k.shape[1]

    w_in_p = _pack_w_in(w_in)
    wa2 = jnp.zeros((depth, 128, 128), F32).at[:, HEADS:HEADS + GLA_RANK, :].set(gla_w_a2).astype(BF16)
    bfrow = jnp.zeros((depth, 1, 128), F32).at[:, 0, 0:HEADS].set(fox_b_f)
    wg_b = w_mgate.astype(BF16)
    wb_b = w_branch.astype(BF16)
    wo_b = w_out.astype(BF16)
    wu_b = ffn_w_up.astype(BF16)
    wd_b = ffn_w_down.astype(BF16)
    wglu_b = ssm_w_glu.astype(BF16)
    ab_re, ab_im, bbcat = _s5_prep(ssm_a_re, ssm_a_im, ssm_log_dt, ssm_b_re, ssm_b_im)
    cccat = _s5_cmat(ssm_c_re, ssm_c_im)
    cos_p, sin_p = _rope_tables(jnp.arange(L))
    cos_s, sin_s = _rope_tables(jnp.full((NS,), past))
    fox_consts = _fox_consts()
    ckt = jnp.transpose(cache_k, (0, 1, 3, 4, 2)).reshape(depth, n_pool, BR, PAGE)
    cvt = jnp.transpose(cache_v, (0, 1, 3, 4, 2)).reshape(depth, n_pool, BR, PAGE)
    clft = jnp.transpose(cache_logf, (0, 1, 3, 2))

    row = lambda a, l: a[l][None, :]
    xp = x_prompt.reshape(T, D_MODEL)
    xs = x_sample.reshape(NS, D_MODEL)
    outs_p, outs_s = [], []
    for l in range(depth):
        (pret, su, pgla, knat, vnat, lf, fq, fk, fv) = _inproj(
            xp, row(norm1_g, l), w_in_p[l], wa2[l], row(gla_b_a, l), bfrow[l], cos_p, sin_p, fox_consts,
            prompt=True, seq_len=L, tm=tm)
        o_ret, s_ret = _retention(pret, row(ret_norm_g, l), batch=B, seq_len=L, tc=tm)
        o_ssm, s_re, s_im = _s5(su, ab_re[l], ab_im[l], bbcat[l], cccat[l], row(ssm_d, l), wglu_b[l],
                                row(ssm_b_glu, l), batch=B, seq_len=L, tm=tm)
        o_gla, s_gla = _gla(pgla, row(gla_norm_g, l), batch=B, seq_len=L, tc=tm)
        o_fox = _fox_prompt(fq, fk, fv, batch=B, seq_len=L, tq=tm)
        xp = _merge(xp, (o_ret, o_ssm, o_gla, o_fox), row(norm1_g, l), wg_b[l], row(b_mgate, l), wb_b[l], wo_b[l], tm=tm)
        xp, tail = _ffn(xp, row(norm2_g, l), wu_b[l], ffn_conv_w[l], row(ffn_conv_b, l), wd_b[l],
                        prompt=True, batch=B, seq_len=L, tm=tm)
        sr = jnp.transpose(s_ret.reshape(B, 2, HEADS, 32, BR), (0, 2, 1, 3, 4)).reshape(B, BR, BR)
        outs_p.append((_diag_blocks(sr, HEADS, RET_DK, 64), s_re.reshape(B, SSM_G, SSM_P), s_im.reshape(B, SSM_G, SSM_P),
                       _diag_blocks(jnp.transpose(s_gla, (0, 2, 1)), HEADS, GLA_DK, 64),
                       tail[:, SUBLANES - (CONV_W - 1):, :],
                       knat.reshape(B, L, HEADS, FOX_HD), vnat.reshape(B, L, HEADS, FOX_HD),
                       lf[:, :HEADS].reshape(B, L, HEADS)))
        (pret, su, pgla, knat, vnat, lf, fqn) = _inproj(
            xs, row(norm1_g, l), w_in_p[l], wa2[l], row(gla_b_a, l), bfrow[l], cos_s, sin_s, None,
            prompt=False, seq_len=NS, tm=NS)
        (o_ret, o_ssm, o_gla, ret_new, gla_new, re_new, im_new) = _sample_mix(
            pret, pgla, su, state_ret[l], state_gla[l], state_ssm_re[l], state_ssm_im[l],
            ret_norm_g[l], gla_norm_g[l], ab_re[l], ab_im[l], bbcat[l], cccat[l], row(ssm_d, l), wglu_b[l],
            row(ssm_b_glu, l))
        lfs = lf[:, :HEADS]
        o_fox = _fox_sample(page_table, fqn, knat, vnat, lfs, ckt, cvt, clft, l)
        xs = _merge(xs, (o_ret.astype(BF16), o_ssm.astype(BF16), o_gla.astype(BF16), o_fox.astype(BF16)),
                    row(norm1_g, l), wg_b[l], row(b_mgate, l), wb_b[l], wo_b[l], tm=NS)
        xs, a_s = _ffn(xs, row(norm2_g, l), wu_b[l], ffn_conv_w[l], row(ffn_conv_b, l), wd_b[l], state_conv[l],
                       prompt=False, batch=NS, seq_len=NS, tm=NS)
        new_conv = jnp.stack([state_conv[l][:, 1, :], a_s], axis=1)
        outs_s.append((ret_new, re_new, im_new, gla_new, new_conv,
                       knat.reshape(NS, 1, HEADS, FOX_HD), vnat.reshape(NS, 1, HEADS, FOX_HD), lfs.reshape(NS, 1, HEADS)))

    y_prompt = _final_norm(xp, final_norm_g[None, :], tm=tm).reshape(B, L, D_MODEL)
    y_sample = _final_norm(xs, final_norm_g[None, :], tm=NS).reshape(NS, 1, D_MODEL)

    def stk(outs, i):
        return jnp.stack([o[i] for o in outs], axis=0)

    return (y_prompt, y_sample,
            stk(outs_p, 5), stk(outs_p, 6), stk(outs_p, 7),
            stk(outs_s, 5), stk(outs_s, 6), stk(outs_s, 7),
            stk(outs_p, 0), stk(outs_s, 0),
            stk(outs_p, 1), stk(outs_p, 2), stk(outs_s, 1), stk(outs_s, 2),
            stk(outs_p, 3), stk(outs_s, 3),
            stk(outs_p, 4), stk(outs_s, 4))
```

```python
import functools
import math

import jax
import jax.numpy as jnp
import numpy as np
from jax import lax
from jax.experimental import pallas as pl
from jax.experimental.pallas import tpu as pltpu

F32 = jnp.float32
BF16 = jnp.bfloat16

LANES = 128
SUBLANES = 8
VMEM_BYTES_V7X = 64 * 1024 * 1024
VMEM_LIMIT = VMEM_BYTES_V7X - 8 * 1024 * 1024

D_MODEL = 1024
BR = 256
N_BRANCH = 4
HEADS = 4
RET_DK = 64
RET_CHUNK = 128
SSM_G = 16
SSM_GC = 16
SSM_P = 64
SSM_N = SSM_G * SSM_P
GLA_DK = 32
GLA_KW = HEADS * GLA_DK
GLA_RANK = 16
GLA_TAU = 16.0
GLA_CHUNK = 32
FOX_HD = 64
D_FF = 2816
FF_CHUNK = 1408
CONV_W = 3
EPS = 1e-6
ROPE_BASE = 10000.0
PAGE = 128
PAGES_PER_STEP = 8
EXP_CLAMP = 80.0
LOG2E = math.log2(math.e)
FOX_QSCALE = (FOX_HD ** -0.5) * LOG2E
FOX_TQ = 1024

C_RQ, C_RK, C_RV, C_RG, C_SU = 0, 256, 512, 768, 1024
C_GQ, C_GK, C_GV, C_GG = 1280, 1408, 1536, 1792
C_FQ, C_FK, C_FV, C_SM = 2048, 2304, 2560, 2816
W_IN_COLS = 2944
G_V, G_G, G_Q, G_K, G_LA = 0, 256, 512, 640, 768
N_RQ, N_RK, N_RV, N_RG, N_SU = 0, 256, 512, 768, 1024
N_GQ, N_GK, N_GV, N_GG, N_GA = 1280, 1408, 1536, 1792, 2048
N_FQ, N_FK, N_FV, N_FF = 2064, 2320, 2576, 2832


def _cparams(sem):
    return pltpu.CompilerParams(dimension_semantics=sem, vmem_limit_bytes=VMEM_LIMIT)


def _dot(a, b):
    return jnp.dot(a, b, preferred_element_type=F32)


def _dot_nt(a, b):
    return lax.dot_general(a, b, (((1,), (1,)), ((), ())), preferred_element_type=F32)


def _dot_tn(a, b):
    return lax.dot_general(a, b, (((0,), (0,)), ((), ())), preferred_element_type=F32)


def _split3(x):
    hi = x.astype(BF16)
    r = x - hi.astype(F32)
    mid = r.astype(BF16)
    lo = (r - mid.astype(F32)).astype(BF16)
    return hi, mid, lo


def _dot3_left(m, x):
    hi, mid, lo = _split3(x)
    return _dot(m, hi) + _dot(m, mid) + _dot(m, lo)


def _dot3_right(x, m):
    hi, mid, lo = _split3(x)
    return _dot(hi, m) + _dot(mid, m) + _dot(lo, m)


def _dot2_right(x, m):
    hi = x.astype(BF16)
    lo = (x - hi.astype(F32)).astype(BF16)
    return _dot(hi, m) + _dot(lo, m)


def _sigmoid(x):
    return 1.0 / (1.0 + jnp.exp(-x))


def _log_sigmoid(x):
    return jnp.minimum(x, 0.0) - jnp.log(1.0 + jnp.exp(-jnp.abs(x)))


def _gelu_tanh(x):
    return 0.5 * x * (1.0 + jnp.tanh(math.sqrt(2.0 / math.pi) * (x + 0.044715 * (x * x * x))))


def _rms(x, g):
    return x * lax.rsqrt(jnp.mean(x * x, axis=-1, keepdims=True) + EPS) * g


def _iota(shape, dim):
    return lax.broadcasted_iota(jnp.int32, shape, dim)


def _head_avg(width, hd):
    r = _iota((width, width), 0) // hd
    c = _iota((width, width), 1) // hd
    return jnp.where(r == c, 1.0 / hd, 0.0).astype(BF16)


def _full(shape):
    nd = len(shape)
    return pl.BlockSpec(shape, lambda *a, _nd=nd: (0,) * _nd)


def _s5_prep_kernel(are_ref, aim_ref, ldt_ref, bre_ref, bim_ref, abre_ref, abim_ref, bbre_ref, bbim_ref):
    a_re = are_ref[...]
    a_im = aim_ref[...]
    dt = jnp.exp(ldt_ref[...])
    mag = jnp.exp(a_re * dt)
    ab_re = mag * jnp.cos(a_im * dt)
    ab_im = mag * jnp.sin(a_im * dt)
    den = a_re * a_re + a_im * a_im
    f_re = ((ab_re - 1.0) * a_re + ab_im * a_im) / den
    f_im = (ab_im * a_re - (ab_re - 1.0) * a_im) / den
    abre_ref[...] = ab_re
    abim_ref[...] = ab_im
    b_re = bre_ref[...]
    b_im = bim_ref[...]
    bbre_ref[...] = f_re[:, None, :] * b_re - f_im[:, None, :] * b_im
    bbim_ref[...] = f_re[:, None, :] * b_im + f_im[:, None, :] * b_re


def _s5_prep(a_re, a_im, log_dt, b_re, b_im):
    depth = a_re.shape[0]
    r = depth * SSM_G
    are = a_re.reshape(r, SSM_P)
    aim = a_im.reshape(r, SSM_P)
    ldt = jnp.broadcast_to(log_dt.reshape(r, 1), (r, SSM_P))
    bre = jnp.transpose(b_re, (0, 1, 3, 2)).reshape(r, SSM_GC, SSM_P)
    bim = jnp.transpose(b_im, (0, 1, 3, 2)).reshape(r, SSM_GC, SSM_P)
    outs = pl.pallas_call(
        _s5_prep_kernel,
        out_shape=(jax.ShapeDtypeStruct((r, SSM_P), F32), jax.ShapeDtypeStruct((r, SSM_P), F32),
                   jax.ShapeDtypeStruct((r, SSM_GC, SSM_P), F32), jax.ShapeDtypeStruct((r, SSM_GC, SSM_P), F32)),
    )(are, aim, ldt, bre, bim)
    ab_re, ab_im, bb_re, bb_im = outs
    ab_re = ab_re.reshape(depth, 1, SSM_N)
    ab_im = ab_im.reshape(depth, 1, SSM_N)
    eye = jnp.eye(SSM_G, dtype=F32)

    def blockdiag(bb):
        bb = bb.reshape(depth, SSM_G, SSM_GC, SSM_P)
        return (bb[:, :, :, None, :] * eye[None, :, None, :, None]).reshape(depth, BR, SSM_N)

    bbcat = jnp.concatenate([blockdiag(bb_re), blockdiag(bb_im)], axis=-1).astype(BF16)
    return ab_re, ab_im, bbcat


def _s5_cmat(c_re, c_im):
    depth = c_re.shape[0]
    eye = jnp.eye(SSM_G, dtype=F32)

    def blockdiag(c):
        ct = jnp.transpose(c, (0, 1, 3, 2))
        return (ct[:, :, :, None, :] * eye[None, :, None, :, None]).reshape(depth, SSM_N, BR)

    return jnp.concatenate([blockdiag(c_re), -blockdiag(c_im)], axis=1).astype(BF16)


def _inproj_kernel(*refs, prompt, tiles_per_batch, n_alias):
    if prompt:
        (x_ref, g_ref, w_ref, wa2_ref, ba_ref, bf_ref, cos_ref, sin_ref, eall_ref, cq_ref, ck_ref) = refs[:11]
        (pret_ref, su_ref, pgla_ref, kt_ref, vt_ref, lft_ref, fq_ref, fk_ref, fv_ref, carry_ref) = refs[11 + n_alias:]
    else:
        (x_ref, g_ref, w_ref, wa2_ref, ba_ref, bf_ref, cos_ref, sin_ref,
         pret_ref, su_ref, pgla_ref, knat_ref, vnat_ref, lf_ref, fqn_ref) = refs
    x = x_ref[...]
    tm = x.shape[0]
    h = _rms(x, g_ref[...]).astype(BF16)
    proj = _dot(h, w_ref[...])
    cos = cos_ref[...]
    sin = sin_ref[...]
    for base in (C_RQ, C_RK):
        x1 = proj[:, base:base + 128]
        x2 = proj[:, base + 128:base + 256]
        pret_ref[:, base:base + 128] = x1 * cos - x2 * sin
        pret_ref[:, base + 128:base + 256] = x1 * sin + x2 * cos
    pret_ref[:, C_RV:C_SU] = proj[:, C_RV:C_SU]
    su_ref[...] = proj[:, C_SU:C_GQ]
    small = proj[:, C_SM:C_SM + 128]
    la = _log_sigmoid(_dot(small.astype(BF16), wa2_ref[...]) + ba_ref[...]) * (1.0 / GLA_TAU)
    pgla_ref[:, G_V:G_Q] = proj[:, C_GV:C_FQ]
    pgla_ref[:, G_Q:G_K] = proj[:, C_GQ:C_GK]
    pgla_ref[:, G_K:G_LA] = proj[:, C_GK:C_GV] * (GLA_DK ** -0.5)
    pgla_ref[:, G_LA:G_LA + 128] = la
    lane = _iota((tm, 128), 1)
    lf = jnp.where(lane < HEADS, _log_sigmoid(small + bf_ref[...]), 0.0)
    if not prompt:
        lf_ref[...] = lf
        knat_ref[...] = proj[:, C_FK:C_FV]
        vnat_ref[...] = proj[:, C_FV:C_SM]
        fqn_ref[...] = proj[:, C_FQ:C_FK]
        return
    kt_ref[...] = proj[:, C_FK:C_FV].T
    vt_ref[...] = proj[:, C_FV:C_SM].T
    lft_ref[...] = lf.T[0:HEADS, :]
    i = pl.program_id(0)

    @pl.when(i % tiles_per_batch == 0)
    def _():
        carry_ref[...] = jnp.zeros_like(carry_ref)

    tri = (_iota((tm, tm), 0) >= _iota((tm, tm), 1)).astype(BF16)
    c = _dot3_left(tri, lf) + carry_ref[0:1, :]
    carry_ref[...] = jnp.broadcast_to(c[tm - 1:tm, :], carry_ref.shape)
    chi, cmid, clo = _split3(c * LOG2E)
    c3 = (chi.astype(F32) + pltpu.roll(cmid.astype(F32), HEADS, 1)
          + pltpu.roll(clo.astype(F32), 2 * HEADS, 1)).astype(BF16)
    aug = _dot(c3, eall_ref[...])
    cq = cq_ref[...]
    ck = ck_ref[...]
    keep = lane < FOX_HD
    for hd in range(HEADS):
        lo = 128 * (hd // 2)
        bq = proj[:, C_FQ + lo:C_FQ + lo + 128]
        bk = proj[:, C_FK + lo:C_FK + lo + 128]
        bv = proj[:, C_FV + lo:C_FV + lo + 128]
        if hd % 2:
            bq = pltpu.roll(bq, FOX_HD, 1)
            bk = pltpu.roll(bk, FOX_HD, 1)
            bv = pltpu.roll(bv, FOX_HD, 1)
        fq_ref[hd] = (jnp.where(keep, bq, 0.0) + aug[:, 128 * hd:128 * hd + 128] + cq).astype(BF16)
        fk_ref[hd] = (jnp.where(keep, bk, 0.0) + aug[:, 128 * (HEADS + hd):128 * (HEADS + hd) + 128] + ck).astype(BF16)
        fv_ref[hd] = jnp.where(keep, bv, jnp.where(lane == FOX_HD, 1.0, 0.0)).astype(BF16)


def _layer_block(shape, layer):
    nd = len(shape)
    return pl.BlockSpec((None,) + tuple(shape), lambda *a, _l=layer, _nd=nd: (_l,) + (0,) * _nd)


def _inproj(x2d, g, w, wa2, ba, bfrow, cos, sin, consts, layer, stacked=None, *, prompt, seq_len, tm):
    t = x2d.shape[0]
    nt = t // tm
    tpb = max(seq_len // tm, 1)
    depth = w.shape[0]
    row = lambda c: pl.BlockSpec((tm, c), lambda i: (i, 0))
    in_specs = [row(D_MODEL), _full((1, D_MODEL)), _layer_block((D_MODEL, W_IN_COLS), layer),
                _layer_block((128, 128), layer), _full((1, 128)), _full((1, 128)),
                pl.BlockSpec((tm, 128), lambda i: (i % tpb, 0)), pl.BlockSpec((tm, 128), lambda i: (i % tpb, 0))]
    args = [x2d, g, w, wa2, ba, bfrow, cos, sin]
    out_shape = [jax.ShapeDtypeStruct((t, 1024), F32), jax.ShapeDtypeStruct((t, BR), F32),
                 jax.ShapeDtypeStruct((t, 896), F32)]
    out_specs = [row(1024), row(BR), row(896)]
    scratch = []
    aliases = {}
    n_alias = 0
    if prompt:
        eall, cq, ck = consts
        batch = t // seq_len
        in_specs += [_full((128, 8 * 128)), _full((1, 128)), _full((1, 128))]
        args += [eall, cq, ck]
        if stacked is not None:
            n_alias = len(stacked)
            aliases = {len(args) + k: len(out_shape) + k for k in range(n_alias)}
            in_specs += [pl.BlockSpec(memory_space=pl.ANY)] * n_alias
            args += list(stacked)
        tok = lambda r: pl.BlockSpec((None, None, r, tm), lambda i, _l=layer: (_l, i // tpb, 0, i % tpb))
        out_shape += [jax.ShapeDtypeStruct((depth, batch, BR, seq_len), F32)] * 2
        out_shape += [jax.ShapeDtypeStruct((depth, batch, HEADS, seq_len), F32)]
        out_specs += [tok(BR), tok(BR), tok(HEADS)]
        hm = pl.BlockSpec((HEADS, tm, 128), lambda i: (0, i, 0))
        out_shape += [jax.ShapeDtypeStruct((HEADS, t, 128), BF16)] * 3
        out_specs += [hm, hm, hm]
        scratch = [pltpu.VMEM((SUBLANES, 128), F32)]
    else:
        out_shape += [jax.ShapeDtypeStruct((t, BR), F32), jax.ShapeDtypeStruct((t, BR), F32),
                      jax.ShapeDtypeStruct((t, 128), F32), jax.ShapeDtypeStruct((t, BR), F32)]
        out_specs += [row(BR), row(BR), row(128), row(BR)]
    return pl.pallas_call(
        functools.partial(_inproj_kernel, prompt=prompt, tiles_per_batch=tpb, n_alias=n_alias),
        grid=(nt,), in_specs=in_specs, out_specs=out_specs, out_shape=out_shape,
        scratch_shapes=scratch, input_output_aliases=aliases, compiler_params=_cparams(("arbitrary",)),
        name="inproj_prompt" if prompt else "inproj_sample",
    )(*args)


def _ret_log_gamma():
    return [math.log(1.0 - 2.0 ** (-5.0 - h)) for h in range(HEADS)]


def _lane_select(idx, values):
    out = jnp.zeros(idx.shape, F32)
    for h, v in enumerate(values):
        out = jnp.where(idx == h, v, out)
    return out


def _ret_kernel(q_ref, k_ref, v_ref, g_ref, gn_ref, o_ref, s_ref, S, *, nchunk):
    j = pl.program_id(1)
    c = RET_CHUNK

    @pl.when(j == 0)
    def _():
        S[...] = jnp.zeros_like(S)

    lg = _ret_log_gamma()
    lane = _iota((1, BR), 1)
    head_qk = (lane & 127) >> 5
    head_v = lane >> 6
    lg_lane = _lane_select(head_qk, lg)
    t = _iota((c, 1), 0).astype(F32)
    gq = jnp.exp((t + 1.0) * lg_lane)
    gk = jnp.exp((c - 1.0 - t) * lg_lane)
    rowi = _iota((BR, 1), 0)
    head_row = (rowi & 127) >> 5
    gam = jnp.exp(float(c) * _lane_select(head_row, lg))
    bd = head_row == head_v
    dt = (_iota((c, c), 0) - _iota((c, c), 1))
    dmat = jnp.concatenate(
        [jnp.where(dt >= 0, jnp.exp(dt.astype(F32) * lg[h]), 0.0) for h in range(HEADS)], axis=0)
    avg = _head_avg(BR, RET_DK)
    gn = gn_ref[...]
    for ci in range(nchunk):
        rows = slice(ci * c, (ci + 1) * c)
        q = q_ref[rows, :]
        k = k_ref[rows, :]
        vb = v_ref[rows, :].astype(BF16)
        qs = jnp.concatenate([jnp.where(head_qk == h, q, 0.0) for h in range(HEADS)], axis=0).astype(BF16)
        sc = (_dot_nt(qs, k.astype(BF16)) * dmat).astype(BF16)
        pv = _dot(sc, vb)
        o = _dot((q * gq).astype(BF16), S[...].astype(BF16))
        for h in range(HEADS):
            o = o + jnp.where(head_v == h, pv[h * c:(h + 1) * c, :], 0.0)
        S[...] = gam * S[...] + jnp.where(bd, _dot_tn((k * gk).astype(BF16), vb), 0.0)
        oc = o - _dot2_right(o, avg)
        on = oc * lax.rsqrt(_dot2_right(oc * oc, avg) + EPS)
        gt = g_ref[rows, :]
        o_ref[rows, :] = (on * gn * (gt * _sigmoid(gt))).astype(BF16)

    @pl.when(j == pl.num_programs(1) - 1)
    def _():
        s_ref[0] = S[...]


def _retention(pret, gn, *, batch, seq_len, tc):
    t = pret.shape[0]
    nj = seq_len // tc
    col = lambda cb: pl.BlockSpec((tc, BR), lambda b, j, _cb=cb: (b * nj + j, _cb))
    return pl.pallas_call(
        functools.partial(_ret_kernel, nchunk=tc // RET_CHUNK), name="retention",
        grid=(batch, nj),
        in_specs=[col(0), col(1), col(2), col(3), _full((1, BR))],
        out_specs=[pl.BlockSpec((tc, BR), lambda b, j: (b * nj + j, 0)),
                   pl.BlockSpec((1, BR, BR), lambda b, j: (b, 0, 0))],
        out_shape=[jax.ShapeDtypeStruct((t, BR), BF16), jax.ShapeDtypeStruct((batch, BR, BR), F32)],
        scratch_shapes=[pltpu.VMEM((BR, BR), F32)],
        compiler_params=_cparams(("arbitrary", "arbitrary")),
    )(pret, pret, pret, pret, gn)


def _cmul(ar, ai, br, bi):
    return ar * br - ai * bi, ar * bi + ai * br


def _s5_kernel(u_ref, abre_ref, abim_ref, bb_ref, cc_ref, d_ref, wg_ref, bg_ref,
               y_ref, sre_ref, sim_ref, xr_ref, xi_ref, car_ref):
    j = pl.program_id(1)
    tm = u_ref.shape[0]

    @pl.when(j == 0)
    def _():
        car_ref[...] = jnp.zeros_like(car_ref)

    a1r = abre_ref[...]
    a1i = abim_ref[...]
    u = u_ref[...]
    bu = _dot(u.astype(BF16), bb_ref[...])
    xr = bu[:, :SSM_N]
    xi = bu[:, SSM_N:]
    row8 = _iota((tm, 1), 0) & 7
    pr, pi_ = a1r, a1i
    for k in (1, 2, 4):
        ok = row8 >= k
        sr = jnp.where(ok, pltpu.roll(xr, k, 0), 0.0)
        si = jnp.where(ok, pltpu.roll(xi, k, 0), 0.0)
        dr, di = _cmul(pr, pi_, sr, si)
        xr = xr + dr
        xi = xi + di
        pr, pi_ = _cmul(pr, pi_, pr, pi_)
    r8 = _iota((SUBLANES, 1), 0)
    tr = jnp.broadcast_to(a1r, (SUBLANES, SSM_N))
    ti = jnp.broadcast_to(a1i, (SUBLANES, SSM_N))
    qr, qi = a1r, a1i
    for bit in range(3):
        nr, ni = _cmul(tr, ti, qr, qi)
        use = ((r8 >> bit) & 1) == 1
        tr = jnp.where(use, nr, tr)
        ti = jnp.where(use, ni, ti)
        qr, qi = _cmul(qr, qi, qr, qi)
    xr_ref[...] = xr
    xi_ref[...] = xi

    def body(g, carry):
        cr, ci = carry
        rows = pl.ds(pl.multiple_of(g * SUBLANES, SUBLANES), SUBLANES)
        dr, di = _cmul(tr, ti, cr, ci)
        r2 = xr_ref[rows, :] + dr
        i2 = xi_ref[rows, :] + di
        xr_ref[rows, :] = r2
        xi_ref[rows, :] = i2
        return r2[SUBLANES - 1:SUBLANES, :], i2[SUBLANES - 1:SUBLANES, :]

    cr, ci = lax.fori_loop(0, tm // SUBLANES, body, (car_ref[0:1, :], car_ref[1:2, :]))
    car_ref[0:1, :] = cr
    car_ref[1:2, :] = ci
    sre_ref[0] = cr
    sim_ref[0] = ci
    xcat = jnp.concatenate([xr_ref[...].astype(BF16), xi_ref[...].astype(BF16)], axis=1)
    y = _dot(xcat, cc_ref[...]) + d_ref[...] * u
    yg = _gelu_tanh(y)
    y_ref[...] = (yg * _sigmoid(_dot(yg.astype(BF16), wg_ref[...]) + bg_ref[...])).astype(BF16)


def _s5(su, ab_re, ab_im, bbcat, cccat, d_skip, w_glu, b_glu, layer, *, batch, seq_len, tm):
    t = su.shape[0]
    nj = seq_len // tm
    return pl.pallas_call(
        _s5_kernel, name="s5_scan",
        grid=(batch, nj),
        in_specs=[pl.BlockSpec((tm, BR), lambda b, j: (b * nj + j, 0)), _full((1, SSM_N)), _full((1, SSM_N)),
                  _layer_block((BR, 2 * SSM_N), layer), _layer_block((2 * SSM_N, BR), layer), _full((1, BR)),
                  _layer_block((BR, BR), layer), _full((1, BR))],
        out_specs=[pl.BlockSpec((tm, BR), lambda b, j: (b * nj + j, 0)),
                   pl.BlockSpec((1, 1, SSM_N), lambda b, j: (b, 0, 0)),
                   pl.BlockSpec((1, 1, SSM_N), lambda b, j: (b, 0, 0))],
        out_shape=[jax.ShapeDtypeStruct((t, BR), BF16), jax.ShapeDtypeStruct((batch, 1, SSM_N), F32),
                   jax.ShapeDtypeStruct((batch, 1, SSM_N), F32)],
        scratch_shapes=[pltpu.VMEM((tm, SSM_N), F32), pltpu.VMEM((tm, SSM_N), F32), pltpu.VMEM((SUBLANES, SSM_N), F32)],
        compiler_params=_cparams(("arbitrary", "arbitrary")),
    )(su, ab_re, ab_im, bbcat, cccat, d_skip, w_glu, b_glu)


def _gla_kernel(q_ref, k_ref, la_ref, v_ref, g_ref, gn_ref, o_ref, s_ref, ST, oi_ref, qt_ref, kh_ref, el_ref):
    j = pl.program_id(1)
    tc = q_ref.shape[0]
    c = GLA_CHUNK
    nchunk = tc // c

    @pl.when(j == 0)
    def _():
        ST[...] = jnp.zeros_like(ST)

    r = _iota((tc, tc), 0)
    s = _iota((tc, tc), 1)
    same = (r // c) == (s // c)
    tri = (same & (s <= r)).astype(BF16)
    ones_blk = same.astype(BF16)
    la = la_ref[...]
    hi, mid, lo = _split3(la)
    b = _dot(tri, hi) + _dot(tri, mid) + _dot(tri, lo)
    bl = _dot(ones_blk, hi) + _dot(ones_blk, mid) + _dot(ones_blk, lo)
    q = q_ref[...]
    k = k_ref[...]
    qt = q * jnp.exp(b)
    kt = k * jnp.exp(jnp.minimum(-b, EXP_CLAMP))
    qt_ref[...] = qt
    kh_ref[...] = k * jnp.exp(bl - b)
    el_ref[...] = jnp.exp(bl)
    lane_k = _iota((1, GLA_KW), 1) >> 5
    lane_v = _iota((1, BR), 1) >> 6
    qs = jnp.concatenate([jnp.where(lane_k == h, qt, 0.0) for h in range(HEADS)], axis=0).astype(BF16)
    sc = _dot_nt(qs, kt.astype(BF16))
    keep = jnp.concatenate([same & (s <= r)] * HEADS, axis=0)
    sc = jnp.where(keep, sc, 0.0).astype(BF16)
    vb = v_ref[...].astype(BF16)
    pv = _dot(sc, vb)
    o = jnp.zeros((tc, BR), F32)
    for h in range(HEADS):
        o = o + jnp.where(lane_v == h, pv[h * tc:(h + 1) * tc, :], 0.0)
    oi_ref[...] = o
    bd = (_iota((BR, 1), 0) >> 6) == lane_k

    st = ST[...]
    for ci in range(nchunk):
        rows = slice(ci * c, (ci + 1) * c)
        oi_ref[rows, :] = oi_ref[rows, :] + _dot_nt(qt_ref[rows, :].astype(BF16), st.astype(BF16))
        upd = _dot_tn(v_ref[rows, :].astype(BF16), kh_ref[rows, :].astype(BF16))
        st = el_ref[(ci + 1) * c - 1:(ci + 1) * c, :] * st + jnp.where(bd, upd, 0.0)
    ST[...] = st
    o = oi_ref[...]
    avg = _head_avg(BR, BR // HEADS)
    on = o * lax.rsqrt(_dot2_right(o * o, avg) + EPS)
    gt = g_ref[...]
    o_ref[...] = (on * gn_ref[...] * (gt * _sigmoid(gt))).astype(BF16)

    @pl.when(j == pl.num_programs(1) - 1)
    def _():
        s_ref[0] = ST[...]


def _gla(pgla, gn, *, batch, seq_len, tc):
    t = pgla.shape[0]
    nj = seq_len // tc
    c128 = lambda cb: pl.BlockSpec((tc, 128), lambda b, j, _cb=cb: (b * nj + j, _cb))
    c256 = lambda cb: pl.BlockSpec((tc, BR), lambda b, j, _cb=cb: (b * nj + j, _cb))
    return pl.pallas_call(
        _gla_kernel, name="gla",
        grid=(batch, nj),
        in_specs=[c128(G_Q // 128), c128(G_K // 128), c128(G_LA // 128), c256(G_V // BR), c256(G_G // BR),
                  _full((1, BR))],
        out_specs=[pl.BlockSpec((tc, BR), lambda b, j: (b * nj + j, 0)),
                   pl.BlockSpec((1, BR, GLA_KW), lambda b, j: (b, 0, 0))],
        out_shape=[jax.ShapeDtypeStruct((t, BR), BF16), jax.ShapeDtypeStruct((batch, BR, GLA_KW), F32)],
        scratch_shapes=[pltpu.VMEM((BR, GLA_KW), F32), pltpu.VMEM((tc, BR), F32), pltpu.VMEM((tc, GLA_KW), F32),
                        pltpu.VMEM((tc, GLA_KW), F32), pltpu.VMEM((tc, GLA_KW), F32)],
        compiler_params=_cparams(("arbitrary", "arbitrary")),
    )(pgla, pgla, pgla, pgla, pgla, gn)


NEG = -1e30


def _fox_kernel(qi_tab, kj_tab, q_ref, k_ref, v_ref, o_ref, m_ref, acc_ref):
    p = pl.program_id(1)
    qi = qi_tab[p]
    kj = kj_tab[p]
    tq = q_ref.shape[1]
    ntile = tq // 128

    @pl.when(kj == 0)
    def _():
        m_ref[...] = jnp.full_like(m_ref, NEG)
        acc_ref[...] = jnp.zeros_like(acc_ref)

    def step(diag):
        if diag:
            causal = _iota((tq, tq), 1) <= _iota((tq, tq), 0)
        for h in range(HEADS):
            s = _dot_nt(q_ref[h], k_ref[h])
            if diag:
                s = jnp.where(causal, s, NEG)
            tiles = [s[:, 128 * c:128 * (c + 1)] for c in range(ntile)]
            m4 = functools.reduce(jnp.maximum, tiles)
            m_prev = m_ref[h]
            m_new = jnp.maximum(m_prev, jnp.max(m4, axis=1, keepdims=True))
            alpha = jnp.exp2(m_prev - m_new)
            pe = jnp.concatenate([jnp.exp2(t - m_new).astype(BF16) for t in tiles], axis=1)
            acc_ref[h] = alpha * acc_ref[h] + _dot(pe, v_ref[h])
            m_ref[h] = m_new

    @pl.when(kj != qi)
    def _():
        step(False)

    @pl.when(kj == qi)
    def _():
        step(True)
        low = _iota((tq, 128), 1) < FOX_HD
        for pair in range(HEADS // 2):
            a0 = acc_ref[2 * pair]
            a1 = acc_ref[2 * pair + 1]
            oe = jnp.where(low, a0 / a0[:, FOX_HD:FOX_HD + 1], 0.0)
            oo = jnp.where(low, a1 / a1[:, FOX_HD:FOX_HD + 1], 0.0)
            o_ref[:, 128 * pair:128 * pair + 128] = (oe + pltpu.roll(oo, FOX_HD, 1)).astype(BF16)


def _fox_prompt(fq, fk, fv, *, batch, seq_len, tq):
    t = fq.shape[1]
    nq = seq_len // tq
    pairs = [(i, j) for i in range(nq) for j in range(i + 1)]
    qi_tab = jnp.asarray([p[0] for p in pairs], jnp.int32)
    kj_tab = jnp.asarray([p[1] for p in pairs], jnp.int32)
    grid_spec = pltpu.PrefetchScalarGridSpec(
        num_scalar_prefetch=2, grid=(batch, len(pairs)),
        in_specs=[pl.BlockSpec((HEADS, tq, 128), lambda b, p, qt, kt: (0, b * nq + qt[p], 0)),
                  pl.BlockSpec((HEADS, tq, 128), lambda b, p, qt, kt: (0, b * nq + kt[p], 0)),
                  pl.BlockSpec((HEADS, tq, 128), lambda b, p, qt, kt: (0, b * nq + kt[p], 0))],
        out_specs=pl.BlockSpec((tq, BR), lambda b, p, qt, kt: (b * nq + qt[p], 0)),
        scratch_shapes=[pltpu.VMEM((HEADS, tq, 128), F32)] * 2)
    return pl.pallas_call(
        _fox_kernel, grid_spec=grid_spec, out_shape=jax.ShapeDtypeStruct((t, BR), BF16), name="fox_prompt",
        compiler_params=_cparams(("arbitrary", "arbitrary")),
    )(qi_tab, kj_tab, fq, fk, fv)


def _merge_kernel(x_ref, o0_ref, o1_ref, o2_ref, o3_ref, g_ref, wg_ref, bg_ref, wb_ref, wo_ref, out_ref):
    x = x_ref[...]
    h = _rms(x, g_ref[...]).astype(BF16)
    merged = None
    for n, o_ref in enumerate((o0_ref, o1_ref, o2_ref, o3_ref)):
        cols = slice(n * D_MODEL, (n + 1) * D_MODEL)
        gate = _sigmoid(_dot(h, wg_ref[:, cols]) + bg_ref[:, cols])
        term = gate * _dot(o_ref[...], wb_ref[n])
        merged = term if merged is None else merged + term
    out_ref[...] = x + _dot(merged.astype(BF16), wo_ref[...])


def _merge(x2d, branches, g, wg, bg, wb, wo, layer, *, tm):
    t = x2d.shape[0]
    row = lambda c: pl.BlockSpec((tm, c), lambda i: (i, 0))
    return pl.pallas_call(
        _merge_kernel, grid=(t // tm,), name="merge",
        in_specs=[row(D_MODEL), row(BR), row(BR), row(BR), row(BR), _full((1, D_MODEL)),
                  _layer_block((D_MODEL, N_BRANCH * D_MODEL), layer), _full((1, N_BRANCH * D_MODEL)),
                  _layer_block((N_BRANCH, BR, D_MODEL), layer), _layer_block((D_MODEL, D_MODEL), layer)],
        out_specs=row(D_MODEL), out_shape=jax.ShapeDtypeStruct((t, D_MODEL), F32),
        compiler_params=_cparams(("arbitrary",)),
    )(x2d, *branches, g, wg, bg, wb, wo)


def _ffn_kernel(*refs, prompt, tiles_per_batch):
    if prompt:
        x_ref, g_ref, wu_ref, cw_ref, cb_ref, wd_ref, out_ref, tail_ref, prev_ref = refs
    else:
        x_ref, g_ref, wu_ref, cw_ref, cb_ref, wd_ref, c0_ref, c1_ref, out_ref, a_ref = refs
    x = x_ref[...]
    tm = x.shape[0]
    h = _rms(x, g_ref[...]).astype(BF16)
    if prompt:
        i = pl.program_id(0)

        @pl.when(i % tiles_per_batch == 0)
        def _():
            prev_ref[...] = jnp.zeros_like(prev_ref)

        rowid = _iota((tm, 1), 0)
    y = x
    for ch in range(D_FF // FF_CHUNK):
        cols = slice(ch * FF_CHUNK, (ch + 1) * FF_CHUNK)
        gcols = slice(D_FF + ch * FF_CHUNK, D_FF + (ch + 1) * FF_CHUNK)
        a = _dot(h, wu_ref[:, cols])
        gate = _dot(h, wu_ref[:, gcols])
        if prompt:
            p1 = prev_ref[SUBLANES - 1:SUBLANES, cols]
            p2 = prev_ref[SUBLANES - 2:SUBLANES - 1, cols]
            a1 = jnp.where(rowid == 0, p1, pltpu.roll(a, 1, 0))
            a2 = jnp.where(rowid == 0, p2, jnp.where(rowid == 1, p1, pltpu.roll(a, 2, 0)))
            prev_ref[:, cols] = a[tm - SUBLANES:tm, :]
            tail_ref[0, :, cols] = a[tm - SUBLANES:tm, :]
        else:
            a1 = c1_ref[:, cols]
            a2 = c0_ref[:, cols]
            a_ref[:, cols] = a
        conv = cb_ref[:, cols] + cw_ref[0:1, cols] * a2 + cw_ref[1:2, cols] * a1 + cw_ref[2:3, cols] * a
        act = (_gelu_tanh(conv) * gate).astype(BF16)
        y = y + _dot(act, wd_ref[cols, :])
    out_ref[...] = y


def _ffn(x2d, g, wu, cw, cb, wd, layer, conv0=None, *, prompt, batch, seq_len, tm):
    t = x2d.shape[0]
    tpb = max(seq_len // tm, 1)
    row = lambda c: pl.BlockSpec((tm, c), lambda i: (i, 0))
    in_specs = [row(D_MODEL), _full((1, D_MODEL)), _layer_block((D_MODEL, 2 * D_FF), layer),
                _layer_block((CONV_W, D_FF), layer), _full((1, D_FF)), _layer_block((D_FF, D_MODEL), layer)]
    args = [x2d, g, wu, cw, cb, wd]
    if prompt:
        out_shape = [jax.ShapeDtypeStruct((t, D_MODEL), F32), jax.ShapeDtypeStruct((batch, SUBLANES, D_FF), F32)]
        out_specs = [row(D_MODEL), pl.BlockSpec((1, SUBLANES, D_FF), lambda i: (i // tpb, 0, 0))]
        scratch = [pltpu.VMEM((SUBLANES, D_FF), F32)]
    else:
        in_specs += [row(D_FF), row(D_FF)]
        args += [conv0[:, 0, :], conv0[:, 1, :]]
        out_shape = [jax.ShapeDtypeStruct((t, D_MODEL), F32), jax.ShapeDtypeStruct((t, D_FF), F32)]
        out_specs = [row(D_MODEL), row(D_FF)]
        scratch = []
    return pl.pallas_call(
        functools.partial(_ffn_kernel, prompt=prompt, tiles_per_batch=tpb), name="conv_ffn",
        grid=(t // tm,), in_specs=in_specs, out_specs=out_specs, out_shape=out_shape,
        scratch_shapes=scratch, compiler_params=_cparams(("arbitrary",)),
    )(*args)


def _norm_kernel(x_ref, g_ref, o_ref):
    o_ref[...] = _rms(x_ref[...], g_ref[...])


def _final_norm(x2d, g, *, tm):
    t = x2d.shape[0]
    row = pl.BlockSpec((tm, D_MODEL), lambda i: (i, 0))
    return pl.pallas_call(
        _norm_kernel, grid=(t // tm,), in_specs=[row, _full((1, D_MODEL))], out_specs=row,
        out_shape=jax.ShapeDtypeStruct((t, D_MODEL), F32), compiler_params=_cparams(("arbitrary",)),
    )(x2d, g)


def _sample_mix_kernel(rq_ref, rk_ref, rv_ref, rg_ref, rs_ref, rgn_ref,
                       gq_ref, gk_ref, gla_ref, gv_ref, gg_ref, gs_ref, ggn_ref,
                       u_ref, sre_ref, sim_ref, abre_ref, abim_ref, bb_ref, cc_ref, d_ref, wg_ref, bg_ref,
                       oret_ref, rsn_ref, ogla_ref, gsn_ref, ossm_ref, sren_ref, simn_ref):
    lg = _ret_log_gamma()
    v = rv_ref[...]
    for h in range(HEADS):
        sn = math.exp(lg[h]) * rs_ref[:, h] + rk_ref[:, h] * v[:, h]
        rsn_ref[:, h] = sn
        o = jnp.sum(rq_ref[:, h] * sn, axis=1)
        oc = o - jnp.mean(o, axis=-1, keepdims=True)
        on = oc * lax.rsqrt(jnp.mean(oc * oc, axis=-1, keepdims=True) + EPS)
        gt = rg_ref[:, h]
        oret_ref[:, h] = on * rgn_ref[h] * (gt * _sigmoid(gt))
    gv = gv_ref[...]
    for h in range(HEADS):
        sn = jnp.exp(gla_ref[:, h]) * gs_ref[:, h] + gk_ref[:, h] * gv[:, h]
        gsn_ref[:, h] = sn
        o = jnp.sum(gq_ref[:, h] * sn, axis=1)
        on = o * lax.rsqrt(jnp.mean(o * o, axis=-1, keepdims=True) + EPS)
        gt = gg_ref[:, h]
        ogla_ref[:, h] = on * ggn_ref[h] * (gt * _sigmoid(gt))
    u = u_ref[...]
    bu = _dot(u.astype(BF16), bb_ref[...])
    dr, di = _cmul(abre_ref[...], abim_ref[...], sre_ref[...], sim_ref[...])
    xr = bu[:, :SSM_N] + dr
    xi = bu[:, SSM_N:] + di
    sren_ref[...] = xr
    simn_ref[...] = xi
    xcat = jnp.concatenate([xr.astype(BF16), xi.astype(BF16)], axis=1)
    y = _dot(xcat, cc_ref[...]) + d_ref[...] * u
    yg = _gelu_tanh(y)
    ossm_ref[...] = yg * _sigmoid(_dot(yg.astype(BF16), wg_ref[...]) + bg_ref[...])


def _sample_mix(pret, pgla, su, st_ret, st_gla, st_re, st_im, rgn, ggn, ab_re, ab_im, bbcat, cccat, d_skip, w_glu, b_glu):
    n = pret.shape[0]

    def unpack_qk(a):
        return jnp.transpose(a.reshape(n, 2, HEADS, 32), (0, 2, 1, 3)).reshape(n, HEADS, RET_DK, 1)

    rq = unpack_qk(pret[:, 0:256])
    rk = unpack_qk(pret[:, 256:512])
    rv = pret[:, 512:768].reshape(n, HEADS, 1, 64)
    rg = pret[:, 768:1024].reshape(n, HEADS, 64)
    gq = pgla[:, G_Q:G_K].reshape(n, HEADS, GLA_DK, 1)
    gk = pgla[:, G_K:G_LA].reshape(n, HEADS, GLA_DK, 1)
    gla = pgla[:, G_LA:G_LA + 128].reshape(n, HEADS, GLA_DK, 1)
    gv = pgla[:, G_V:G_G].reshape(n, HEADS, 1, 64)
    gg = pgla[:, G_G:G_Q].reshape(n, HEADS, 64)
    out_shape = [jax.ShapeDtypeStruct((n, HEADS, 64), F32), jax.ShapeDtypeStruct(st_ret.shape, F32),
                 jax.ShapeDtypeStruct((n, HEADS, 64), F32), jax.ShapeDtypeStruct(st_gla.shape, F32),
                 jax.ShapeDtypeStruct((n, BR), F32), jax.ShapeDtypeStruct((n, SSM_N), F32),
                 jax.ShapeDtypeStruct((n, SSM_N), F32)]
    outs = pl.pallas_call(
        _sample_mix_kernel, out_shape=out_shape, name="sample_mixers",
        compiler_params=pltpu.CompilerParams(vmem_limit_bytes=VMEM_LIMIT),
    )(rq, rk, rv, rg, st_ret, rgn.reshape(HEADS, 1, 64), gq, gk, gla, gv, gg, st_gla, ggn.reshape(HEADS, 1, 64),
      su, st_re.reshape(n, SSM_N), st_im.reshape(n, SSM_N), ab_re, ab_im, bbcat, cccat, d_skip, w_glu, b_glu)
    o_ret, ret_new, o_gla, gla_new, o_ssm, re_new, im_new = outs
    return (o_ret.reshape(n, BR), o_ssm, o_gla.reshape(n, BR), ret_new, gla_new,
            re_new.reshape(n, SSM_G, SSM_P), im_new.reshape(n, SSM_G, SSM_P))


def _fox_sample_kernel(*refs, npp):
    pt_ref = refs[0]
    q_ref, kn_ref, vn_ref, lfn_ref = refs[1:5]
    k_refs = refs[5:5 + npp]
    v_refs = refs[5 + npp:5 + 2 * npp]
    lf_refs = refs[5 + 2 * npp:5 + 3 * npp]
    o_ref, qb_ref, m_ref, l_ref, acc_ref, car_ref = refs[5 + 3 * npp:]
    p = pl.program_id(1)
    lane4 = _iota((HEADS, PAGE), 1)
    row4 = _iota((HEADS, PAGE), 0)

    def head_rows(x):
        out = jnp.zeros((HEADS, x.shape[1]), F32)
        for h in range(HEADS):
            r = jnp.sum(x[FOX_HD * h:FOX_HD * (h + 1), :], axis=0, keepdims=True)
            out = jnp.where(_iota(out.shape, 0) == h, r, out)
        return out

    @pl.when(p == 0)
    def _():
        qcol = q_ref[0]
        qb_ref[...] = jnp.broadcast_to(qcol, qb_ref.shape)
        m_ref[...] = jnp.broadcast_to(head_rows(qcol * kn_ref[0]), m_ref.shape)
        l_ref[...] = jnp.ones_like(l_ref)
        acc_ref[...] = jnp.where(_iota(acc_ref.shape, 1) == 0, vn_ref[0], 0.0)
        car_ref[...] = jnp.broadcast_to(lfn_ref[0], car_ref.shape)

    qb = qb_ref[...]
    car = car_ref[...]
    logits = []
    lf_all = jnp.concatenate([lf_refs[i][0, 0] for i in range(npp)], axis=0)
    upper = (_iota((PAGE, PAGE), 0) >= _iota((PAGE, PAGE), 1)).astype(BF16)
    suf_all = _dot3_right(lf_all, upper)
    for i in range(npp):
        lf = lf_all[SUBLANES * i:SUBLANES * i + HEADS, :]
        suf = suf_all[SUBLANES * i:SUBLANES * i + HEADS, :]
        s = head_rows(k_refs[i][0, 0] * qb) + ((suf - lf) + car) * LOG2E
        car = car + suf[:, 0:1]
        logits.append(s)
    m_prev = m_ref[...]
    m_new = jnp.maximum(m_prev, jnp.max(functools.reduce(jnp.maximum, logits), axis=1, keepdims=True))
    alpha = jnp.exp2(m_prev - m_new)
    probs = [jnp.exp2(s - m_new) for s in logits]
    l_ref[...] = alpha * l_ref[...] + jnp.sum(functools.reduce(jnp.add, probs), axis=1, keepdims=True)
    m_ref[...] = m_new
    car_ref[...] = car
    for h in range(HEADS):
        rows = slice(FOX_HD * h, FOX_HD * (h + 1))
        a = alpha[h:h + 1, :] * acc_ref[rows, :]
        for i in range(npp):
            a = a + v_refs[i][0, 0, rows, :] * probs[i][h:h + 1, :]
        acc_ref[rows, :] = a

    @pl.when(p == pl.num_programs(1) - 1)
    def _():
        for h in range(HEADS):
            rows = slice(FOX_HD * h, FOX_HD * (h + 1))
            o_ref[0, rows, :] = jnp.sum(acc_ref[rows, :], axis=1, keepdims=True) / l_ref[h:h + 1, 0:1]


def _fox_sample(page_table, fq, kn, vn, lfn, cache_kt, cache_vt, cache_lft, layer):
    n, n_pages = page_table.shape
    npp = PAGES_PER_STEP
    while n_pages % npp:
        npp //= 2
    steps = n_pages // npp
    pt = page_table.reshape(-1)

    def page_spec(shape, i):
        return pl.BlockSpec(
            shape, lambda b, p, pt_, _i=i: (layer, pt_[b * n_pages + n_pages - 1 - (p * npp + _i)], 0, 0))

    col3 = lambda r: pl.BlockSpec((1, r, 1), lambda b, p, pt_: (b, 0, 0))
    in_specs = ([col3(BR), col3(BR), col3(BR), col3(HEADS)]
                + [page_spec((1, 1, BR, PAGE), i) for i in range(npp)]
                + [page_spec((1, 1, BR, PAGE), i) for i in range(npp)]
                + [page_spec((1, 1, SUBLANES, PAGE), i) for i in range(npp)])
    grid_spec = pltpu.PrefetchScalarGridSpec(
        num_scalar_prefetch=1, grid=(n, steps), in_specs=in_specs, out_specs=col3(BR),
        scratch_shapes=[pltpu.VMEM((BR, PAGE), F32), pltpu.VMEM((HEADS, PAGE), F32), pltpu.VMEM((HEADS, PAGE), F32),
                        pltpu.VMEM((BR, PAGE), F32), pltpu.VMEM((HEADS, PAGE), F32)])
    out = pl.pallas_call(
        functools.partial(_fox_sample_kernel, npp=npp), grid_spec=grid_spec,
        out_shape=jax.ShapeDtypeStruct((n, BR, 1), F32), name="fox_sample",
        compiler_params=_cparams(("arbitrary", "arbitrary")),
    )(pt, fq[:, :, None], kn[:, :, None], vn[:, :, None], lfn[:, :, None],
      *([cache_kt] * npp), *([cache_vt] * npp), *([cache_lft] * npp))
    return out.reshape(n, BR)


def _pack_w_in(w):
    def rope_pack(blk):
        d = blk.shape[0]
        b4 = blk.reshape(d, D_MODEL, HEADS, 2, 32)
        return jnp.transpose(b4, (0, 1, 3, 2, 4)).reshape(d, D_MODEL, 256)

    depth = w.shape[0]
    small = jnp.zeros((depth, D_MODEL, 128), w.dtype)
    small = small.at[:, :, 0:HEADS].set(w[:, :, N_FF:N_FF + HEADS])
    small = small.at[:, :, HEADS:HEADS + GLA_RANK].set(w[:, :, N_GA:N_GA + GLA_RANK])
    parts = [rope_pack(w[:, :, N_RQ:N_RK]), rope_pack(w[:, :, N_RK:N_RV]) * (RET_DK ** -0.5),
             w[:, :, N_RV:N_GQ],
             w[:, :, N_GQ:N_GA],
             w[:, :, N_FQ:N_FK] * FOX_QSCALE, w[:, :, N_FK:N_FF], small]
    return jnp.concatenate(parts, axis=-1).astype(BF16)


def _rope_tables(pos):
    inv = ROPE_BASE ** (-jnp.arange(32, dtype=F32) * 2.0 / RET_DK)
    ang = pos.astype(F32)[:, None] * inv[None, :]
    return jnp.tile(jnp.cos(ang), (1, HEADS)), jnp.tile(jnp.sin(ang), (1, HEADS))


def _fox_consts():
    eall = np.zeros((128, 8 * 128), np.float32)
    for h in range(HEADS):
        for part in range(3):
            eall[part * HEADS + h, 128 * h + FOX_HD + part] = 1.0
            eall[part * HEADS + h, 128 * (HEADS + h) + FOX_HD + 3 + part] = -1.0
    cq = np.zeros((1, 128), np.float32)
    ck = np.zeros((1, 128), np.float32)
    cq[0, FOX_HD + 3:FOX_HD + 6] = 1.0
    ck[0, FOX_HD:FOX_HD + 3] = 1.0
    return jnp.asarray(eall, BF16), jnp.asarray(cq), jnp.asarray(ck)


def _diag_blocks(s, nh, dk, dv):
    b = s.shape[0]
    s5 = s.reshape(b, nh, dk, nh, dv)
    idx = jnp.arange(nh)
    return jnp.transpose(s5[:, idx, :, idx, :], (1, 0, 2, 3))


def kernel(x_prompt, x_sample, cache_k, cache_v, cache_logf, state_ret, state_ssm_re, state_ssm_im, state_gla, state_conv, page_table, norm1_g, w_in, w_mgate, b_mgate, ret_norm_g, ssm_a_re, ssm_a_im, ssm_log_dt, ssm_b_re, ssm_b_im, ssm_c_re, ssm_c_im, ssm_d, ssm_w_glu, ssm_b_glu, gla_w_a2, gla_b_a, gla_norm_g, fox_b_f, w_branch, w_out, norm2_g, ffn_w_up, ffn_conv_w, ffn_conv_b, ffn_w_down, final_norm_g):
    depth = w_in.shape[0]
    B, L, _ = x_prompt.shape
    NS = x_sample.shape[0]
    n_pages = page_table.shape[1]
    past = n_pages * PAGE
    T = B * L
    tm = min(512, L)
    n_pool = cache_k.shape[1]

    w_in_p = _pack_w_in(w_in)
    wa2 = jnp.zeros((depth, 128, 128), F32).at[:, HEADS:HEADS + GLA_RANK, :].set(gla_w_a2).astype(BF16)
    bfrow = jnp.zeros((depth, 1, 128), F32).at[:, 0, 0:HEADS].set(fox_b_f)
    wg_b = w_mgate.astype(BF16)
    wb_b = w_branch.astype(BF16)
    wo_b = w_out.astype(BF16)
    wu_b = ffn_w_up.astype(BF16)
    wd_b = ffn_w_down.astype(BF16)
    wglu_b = ssm_w_glu.astype(BF16)
    ab_re, ab_im, bbcat = _s5_prep(ssm_a_re, ssm_a_im, ssm_log_dt, ssm_b_re, ssm_b_im)
    cccat = _s5_cmat(ssm_c_re, ssm_c_im)
    cos_p, sin_p = _rope_tables(jnp.arange(L))
    cos_s, sin_s = _rope_tables(jnp.full((NS,), past))
    fox_consts = _fox_consts()
    ckt = jnp.transpose(cache_k, (0, 1, 3, 4, 2)).reshape(depth, n_pool, BR, PAGE)
    cvt = jnp.transpose(cache_v, (0, 1, 3, 4, 2)).reshape(depth, n_pool, BR, PAGE)
    clft = jnp.pad(jnp.transpose(cache_logf, (0, 1, 3, 2)), ((0, 0), (0, 0), (0, SUBLANES - HEADS), (0, 0)))

    row = lambda a, l: a[l][None, :]
    xp = x_prompt.reshape(T, D_MODEL)
    xs = x_sample.reshape(NS, D_MODEL)
    outs_p, outs_s = [], []
    stacked = None
    for l in range(depth):
        (pret, su, pgla, kt_all, vt_all, lft_all, fq, fk, fv) = _inproj(
            xp, row(norm1_g, l), w_in_p, wa2, row(gla_b_a, l), bfrow[l], cos_p, sin_p, fox_consts, l, stacked,
            prompt=True, seq_len=L, tm=tm)
        stacked = (kt_all, vt_all, lft_all)
        o_ret, s_ret = _retention(pret, row(ret_norm_g, l), batch=B, seq_len=L, tc=tm)
        o_ssm, s_re, s_im = _s5(su, ab_re[l], ab_im[l], bbcat, cccat, row(ssm_d, l), wglu_b,
                                row(ssm_b_glu, l), l, batch=B, seq_len=L, tm=tm)
        o_gla, s_gla = _gla(pgla, row(gla_norm_g, l), batch=B, seq_len=L, tc=tm)
        o_fox = _fox_prompt(fq, fk, fv, batch=B, seq_len=L, tq=min(FOX_TQ, L))
        xp = _merge(xp, (o_ret, o_ssm, o_gla, o_fox), row(norm1_g, l), wg_b, row(b_mgate, l), wb_b, wo_b, l, tm=tm)
        xp, tail = _ffn(xp, row(norm2_g, l), wu_b, ffn_conv_w, row(ffn_conv_b, l), wd_b, l,
                        prompt=True, batch=B, seq_len=L, tm=tm)
        sr = jnp.transpose(s_ret.reshape(B, 2, HEADS, 32, BR), (0, 2, 1, 3, 4)).reshape(B, BR, BR)
        outs_p.append((_diag_blocks(sr, HEADS, RET_DK, 64), s_re.reshape(B, SSM_G, SSM_P), s_im.reshape(B, SSM_G, SSM_P),
                       _diag_blocks(jnp.transpose(s_gla, (0, 2, 1)), HEADS, GLA_DK, 64),
                       tail[:, SUBLANES - (CONV_W - 1):, :]))
        (pret, su, pgla, knat, vnat, lf, fqn) = _inproj(
            xs, row(norm1_g, l), w_in_p, wa2, row(gla_b_a, l), bfrow[l], cos_s, sin_s, None, l,
            prompt=False, seq_len=NS, tm=NS)
        (o_ret, o_ssm, o_gla, ret_new, gla_new, re_new, im_new) = _sample_mix(
            pret, pgla, su, state_ret[l], state_gla[l], state_ssm_re[l], state_ssm_im[l],
            ret_norm_g[l], gla_norm_g[l], ab_re[l], ab_im[l], bbcat[l], cccat[l], row(ssm_d, l), wglu_b[l],
            row(ssm_b_glu, l))
        lfs = lf[:, :HEADS]
        o_fox = _fox_sample(page_table, fqn, knat, vnat, lfs, ckt, cvt, clft, l)
        xs = _merge(xs, (o_ret.astype(BF16), o_ssm.astype(BF16), o_gla.astype(BF16), o_fox.astype(BF16)),
                    row(norm1_g, l), wg_b, row(b_mgate, l), wb_b, wo_b, l, tm=NS)
        xs, a_s = _ffn(xs, row(norm2_g, l), wu_b, ffn_conv_w, row(ffn_conv_b, l), wd_b, l, state_conv[l],
                       prompt=False, batch=NS, seq_len=NS, tm=NS)
        new_conv = jnp.stack([state_conv[l][:, 1, :], a_s], axis=1)
        outs_s.append((ret_new, re_new, im_new, gla_new, new_conv,
                       knat.reshape(NS, 1, HEADS, FOX_HD), vnat.reshape(NS, 1, HEADS, FOX_HD), lfs.reshape(NS, 1, HEADS)))

    y_prompt = _final_norm(xp, final_norm_g[None, :], tm=tm).reshape(B, L, D_MODEL)
    y_sample = _final_norm(xs, final_norm_g[None, :], tm=NS).reshape(NS, 1, D_MODEL)

    def stk(outs, i):
        return jnp.stack([o[i] for o in outs], axis=0)

    kt_all, vt_all, lft_all = stacked
    k_prompt = jnp.transpose(kt_all.reshape(depth, B, HEADS, FOX_HD, L), (0, 1, 4, 2, 3))
    v_prompt = jnp.transpose(vt_all.reshape(depth, B, HEADS, FOX_HD, L), (0, 1, 4, 2, 3))
    logf_prompt = jnp.transpose(lft_all, (0, 1, 3, 2))
    return (y_prompt, y_sample,
            k_prompt, v_prompt, logf_prompt,
            stk(outs_s, 5), stk(outs_s, 6), stk(outs_s, 7),
            stk(outs_p, 0), stk(outs_s, 0),
            stk(outs_p, 1), stk(outs_p, 2), stk(outs_s, 1), stk(outs_s, 2),
            stk(outs_p, 3), stk(outs_s, 3),
            stk(outs_p, 4), stk(outs_s, 4))
```

```python
import functools
import math

import jax
import jax.numpy as jnp
import numpy as np
from jax import lax
from jax.experimental import pallas as pl
from jax.experimental.pallas import tpu as pltpu

F32 = jnp.float32
BF16 = jnp.bfloat16

LANES = 128
SUBLANES = 8
VMEM_BYTES_V7X = 64 * 1024 * 1024
VMEM_LIMIT = VMEM_BYTES_V7X - 8 * 1024 * 1024

D_MODEL = 1024
BR = 256
N_BRANCH = 4
HEADS = 4
RET_DK = 64
RET_CHUNK = 128
SSM_G = 16
SSM_GC = 16
SSM_P = 64
SSM_N = SSM_G * SSM_P
GLA_DK = 32
GLA_KW = HEADS * GLA_DK
GLA_RANK = 16
GLA_TAU = 16.0
GLA_CHUNK = 32
FOX_HD = 64
D_FF = 2816
FF_CHUNK = 1408
CONV_W = 3
EPS = 1e-6
ROPE_BASE = 10000.0
PAGE = 128
PAGES_PER_STEP = 8
EXP_CLAMP = 80.0
LOG2E = math.log2(math.e)
FOX_QSCALE = (FOX_HD ** -0.5) * LOG2E
FOX_TQ = 1024

C_RQ, C_RK, C_RV, C_RG, C_SU = 0, 256, 512, 768, 1024
C_GQ, C_GK, C_GV, C_GG = 1280, 1408, 1536, 1792
C_FQ, C_FK, C_FV, C_SM = 2048, 2304, 2560, 2816
W_IN_COLS = 2944
G_V, G_G, G_Q, G_K, G_LA = 0, 256, 512, 640, 768
N_RQ, N_RK, N_RV, N_RG, N_SU = 0, 256, 512, 768, 1024
N_GQ, N_GK, N_GV, N_GG, N_GA = 1280, 1408, 1536, 1792, 2048
N_FQ, N_FK, N_FV, N_FF = 2064, 2320, 2576, 2832


def _cparams(sem):
    return pltpu.CompilerParams(dimension_semantics=sem, vmem_limit_bytes=VMEM_LIMIT)


def _dot(a, b):
    return jnp.dot(a, b, preferred_element_type=F32)


def _dot_nt(a, b):
    return lax.dot_general(a, b, (((1,), (1,)), ((), ())), preferred_element_type=F32)


def _dot_tn(a, b):
    return lax.dot_general(a, b, (((0,), (0,)), ((), ())), preferred_element_type=F32)


def _split3(x):
    hi = x.astype(BF16)
    r = x - hi.astype(F32)
    mid = r.astype(BF16)
    lo = (r - mid.astype(F32)).astype(BF16)
    return hi, mid, lo


def _dot3_left(m, x):
    hi, mid, lo = _split3(x)
    return _dot(m, hi) + _dot(m, mid) + _dot(m, lo)


def _dot3_right(x, m):
    hi, mid, lo = _split3(x)
    return _dot(hi, m) + _dot(mid, m) + _dot(lo, m)


def _dot2_right(x, m):
    hi = x.astype(BF16)
    lo = (x - hi.astype(F32)).astype(BF16)
    return _dot(hi, m) + _dot(lo, m)


def _sigmoid(x):
    return 1.0 / (1.0 + jnp.exp(-x))


def _log_sigmoid(x):
    return jnp.minimum(x, 0.0) - jnp.log(1.0 + jnp.exp(-jnp.abs(x)))


def _gelu_tanh(x):
    return 0.5 * x * (1.0 + jnp.tanh(math.sqrt(2.0 / math.pi) * (x + 0.044715 * (x * x * x))))


def _rms(x, g):
    return x * lax.rsqrt(jnp.mean(x * x, axis=-1, keepdims=True) + EPS) * g


def _iota(shape, dim):
    return lax.broadcasted_iota(jnp.int32, shape, dim)


def _head_avg(width, hd):
    r = _iota((width, width), 0) // hd
    c = _iota((width, width), 1) // hd
    return jnp.where(r == c, 1.0 / hd, 0.0).astype(BF16)


def _full(shape):
    nd = len(shape)
    return pl.BlockSpec(shape, lambda *a, _nd=nd: (0,) * _nd)


def _s5_prep_kernel(are_ref, aim_ref, ldt_ref, bre_ref, bim_ref, abre_ref, abim_ref, bbre_ref, bbim_ref):
    a_re = are_ref[...]
    a_im = aim_ref[...]
    dt = jnp.exp(ldt_ref[...])
    mag = jnp.exp(a_re * dt)
    ab_re = mag * jnp.cos(a_im * dt)
    ab_im = mag * jnp.sin(a_im * dt)
    den = a_re * a_re + a_im * a_im
    f_re = ((ab_re - 1.0) * a_re + ab_im * a_im) / den
    f_im = (ab_im * a_re - (ab_re - 1.0) * a_im) / den
    abre_ref[...] = ab_re
    abim_ref[...] = ab_im
    b_re = bre_ref[...]
    b_im = bim_ref[...]
    bbre_ref[...] = f_re[:, None, :] * b_re - f_im[:, None, :] * b_im
    bbim_ref[...] = f_re[:, None, :] * b_im + f_im[:, None, :] * b_re


def _s5_prep(a_re, a_im, log_dt, b_re, b_im):
    depth = a_re.shape[0]
    r = depth * SSM_G
    are = a_re.reshape(r, SSM_P)
    aim = a_im.reshape(r, SSM_P)
    ldt = jnp.broadcast_to(log_dt.reshape(r, 1), (r, SSM_P))
    bre = jnp.transpose(b_re, (0, 1, 3, 2)).reshape(r, SSM_GC, SSM_P)
    bim = jnp.transpose(b_im, (0, 1, 3, 2)).reshape(r, SSM_GC, SSM_P)
    outs = pl.pallas_call(
        _s5_prep_kernel,
        out_shape=(jax.ShapeDtypeStruct((r, SSM_P), F32), jax.ShapeDtypeStruct((r, SSM_P), F32),
                   jax.ShapeDtypeStruct((r, SSM_GC, SSM_P), F32), jax.ShapeDtypeStruct((r, SSM_GC, SSM_P), F32)),
    )(are, aim, ldt, bre, bim)
    ab_re, ab_im, bb_re, bb_im = outs
    ab_re = ab_re.reshape(depth, 1, SSM_N)
    ab_im = ab_im.reshape(depth, 1, SSM_N)
    eye = jnp.eye(SSM_G, dtype=F32)

    def blockdiag(bb):
        bb = bb.reshape(depth, SSM_G, SSM_GC, SSM_P)
        return (bb[:, :, :, None, :] * eye[None, :, None, :, None]).reshape(depth, BR, SSM_N)

    bbcat = jnp.concatenate([blockdiag(bb_re), blockdiag(bb_im)], axis=-1).astype(BF16)
    return ab_re, ab_im, bbcat


def _s5_cmat(c_re, c_im):
    depth = c_re.shape[0]
    eye = jnp.eye(SSM_G, dtype=F32)

    def blockdiag(c):
        ct = jnp.transpose(c, (0, 1, 3, 2))
        return (ct[:, :, :, None, :] * eye[None, :, None, :, None]).reshape(depth, SSM_N, BR)

    return jnp.concatenate([blockdiag(c_re), -blockdiag(c_im)], axis=1).astype(BF16)


def _inproj_kernel(*refs, prompt, tiles_per_batch, n_alias):
    if prompt:
        (x_ref, g_ref, w_ref, wa2_ref, ba_ref, bf_ref, cos_ref, sin_ref, eall_ref, cq_ref, ck_ref) = refs[:11]
        (pret_ref, su_ref, pgla_ref, kt_ref, vt_ref, lft_ref, fq_ref, fk_ref, fv_ref, carry_ref) = refs[11 + n_alias:]
    else:
        (x_ref, g_ref, w_ref, wa2_ref, ba_ref, bf_ref, cos_ref, sin_ref,
         pret_ref, su_ref, pgla_ref, knat_ref, vnat_ref, lf_ref, fqn_ref) = refs
    x = x_ref[...]
    tm = x.shape[0]
    h = _rms(x, g_ref[...]).astype(BF16)
    proj = _dot(h, w_ref[...])
    cos = cos_ref[...]
    sin = sin_ref[...]
    for base in (C_RQ, C_RK):
        x1 = proj[:, base:base + 128]
        x2 = proj[:, base + 128:base + 256]
        pret_ref[:, base:base + 128] = x1 * cos - x2 * sin
        pret_ref[:, base + 128:base + 256] = x1 * sin + x2 * cos
    pret_ref[:, C_RV:C_SU] = proj[:, C_RV:C_SU]
    su_ref[...] = proj[:, C_SU:C_GQ]
    small = proj[:, C_SM:C_SM + 128]
    la = _log_sigmoid(_dot(small.astype(BF16), wa2_ref[...]) + ba_ref[...]) * (1.0 / GLA_TAU)
    pgla_ref[:, G_V:G_Q] = proj[:, C_GV:C_FQ]
    pgla_ref[:, G_Q:G_K] = proj[:, C_GQ:C_GK]
    pgla_ref[:, G_K:G_LA] = proj[:, C_GK:C_GV] * (GLA_DK ** -0.5)
    pgla_ref[:, G_LA:G_LA + 128] = la
    lane = _iota((tm, 128), 1)
    lf = jnp.where(lane < HEADS, _log_sigmoid(small + bf_ref[...]), 0.0)
    if not prompt:
        lf_ref[...] = lf
        knat_ref[...] = proj[:, C_FK:C_FV]
        vnat_ref[...] = proj[:, C_FV:C_SM]
        fqn_ref[...] = proj[:, C_FQ:C_FK]
        return
    kt_ref[...] = proj[:, C_FK:C_FV].T
    vt_ref[...] = proj[:, C_FV:C_SM].T
    lft_ref[...] = lf.T[0:HEADS, :]
    i = pl.program_id(0)

    @pl.when(i % tiles_per_batch == 0)
    def _():
        carry_ref[...] = jnp.zeros_like(carry_ref)

    tri = (_iota((tm, tm), 0) >= _iota((tm, tm), 1)).astype(BF16)
    c = _dot3_left(tri, lf) + carry_ref[0:1, :]
    carry_ref[...] = jnp.broadcast_to(c[tm - 1:tm, :], carry_ref.shape)
    chi, cmid, clo = _split3(c * LOG2E)
    c3 = (chi.astype(F32) + pltpu.roll(cmid.astype(F32), HEADS, 1)
          + pltpu.roll(clo.astype(F32), 2 * HEADS, 1)).astype(BF16)
    aug = _dot(c3, eall_ref[...])
    cq = cq_ref[...]
    ck = ck_ref[...]
    keep = lane < FOX_HD
    for hd in range(HEADS):
        lo = 128 * (hd // 2)
        bq = proj[:, C_FQ + lo:C_FQ + lo + 128]
        bk = proj[:, C_FK + lo:C_FK + lo + 128]
        bv = proj[:, C_FV + lo:C_FV + lo + 128]
        if hd % 2:
            bq = pltpu.roll(bq, FOX_HD, 1)
            bk = pltpu.roll(bk, FOX_HD, 1)
            bv = pltpu.roll(bv, FOX_HD, 1)
        fq_ref[hd] = (jnp.where(keep, bq, 0.0) + aug[:, 128 * hd:128 * hd + 128] + cq).astype(BF16)
        fk_ref[hd] = (jnp.where(keep, bk, 0.0) + aug[:, 128 * (HEADS + hd):128 * (HEADS + hd) + 128] + ck).astype(BF16)
        fv_ref[hd] = jnp.where(keep, bv, jnp.where(lane == FOX_HD, 1.0, 0.0)).astype(BF16)


def _layer_block(shape, layer):
    nd = len(shape)
    return pl.BlockSpec((None,) + tuple(shape), lambda *a, _l=layer, _nd=nd: (_l,) + (0,) * _nd)


def _inproj(x2d, g, w, wa2, ba, bfrow, cos, sin, consts, layer, stacked=None, *, prompt, seq_len, tm):
    t = x2d.shape[0]
    nt = t // tm
    tpb = max(seq_len // tm, 1)
    depth = w.shape[0]
    row = lambda c: pl.BlockSpec((tm, c), lambda i: (i, 0))
    in_specs = [row(D_MODEL), _full((1, D_MODEL)), _layer_block((D_MODEL, W_IN_COLS), layer),
                _layer_block((128, 128), layer), _full((1, 128)), _full((1, 128)),
                pl.BlockSpec((tm, 128), lambda i: (i % tpb, 0)), pl.BlockSpec((tm, 128), lambda i: (i % tpb, 0))]
    args = [x2d, g, w, wa2, ba, bfrow, cos, sin]
    out_shape = [jax.ShapeDtypeStruct((t, 1024), F32), jax.ShapeDtypeStruct((t, BR), F32),
                 jax.ShapeDtypeStruct((t, 896), F32)]
    out_specs = [row(1024), row(BR), row(896)]
    scratch = []
    aliases = {}
    n_alias = 0
    if prompt:
        eall, cq, ck = consts
        batch = t // seq_len
        in_specs += [_full((128, 8 * 128)), _full((1, 128)), _full((1, 128))]
        args += [eall, cq, ck]
        if stacked is not None:
            n_alias = len(stacked)
            aliases = {len(args) + k: len(out_shape) + k for k in range(n_alias)}
            in_specs += [pl.BlockSpec(memory_space=pl.ANY)] * n_alias
            args += list(stacked)
        tok = lambda r: pl.BlockSpec((None, None, r, tm), lambda i, _l=layer: (_l, i // tpb, 0, i % tpb))
        out_shape += [jax.ShapeDtypeStruct((depth, batch, BR, seq_len), F32)] * 2
        out_shape += [jax.ShapeDtypeStruct((depth, batch, HEADS, seq_len), F32)]
        out_specs += [tok(BR), tok(BR), tok(HEADS)]
        hm = pl.BlockSpec((HEADS, tm, 128), lambda i: (0, i, 0))
        out_shape += [jax.ShapeDtypeStruct((HEADS, t, 128), BF16)] * 3
        out_specs += [hm, hm, hm]
        scratch = [pltpu.VMEM((SUBLANES, 128), F32)]
    else:
        out_shape += [jax.ShapeDtypeStruct((t, BR), F32), jax.ShapeDtypeStruct((t, BR), F32),
                      jax.ShapeDtypeStruct((t, 128), F32), jax.ShapeDtypeStruct((t, BR), F32)]
        out_specs += [row(BR), row(BR), row(128), row(BR)]
    return pl.pallas_call(
        functools.partial(_inproj_kernel, prompt=prompt, tiles_per_batch=tpb, n_alias=n_alias),
        grid=(nt,), in_specs=in_specs, out_specs=out_specs, out_shape=out_shape,
        scratch_shapes=scratch, input_output_aliases=aliases, compiler_params=_cparams(("arbitrary",)),
        name="inproj_prompt" if prompt else "inproj_sample",
    )(*args)


def _ret_log_gamma():
    return [math.log(1.0 - 2.0 ** (-5.0 - h)) for h in range(HEADS)]


def _lane_select(idx, values):
    out = jnp.zeros(idx.shape, F32)
    for h, v in enumerate(values):
        out = jnp.where(idx == h, v, out)
    return out


def _ret_kernel(q_ref, k_ref, v_ref, g_ref, gn_ref, o_ref, s_ref, S, *, nchunk):
    j = pl.program_id(1)
    c = RET_CHUNK

    @pl.when(j == 0)
    def _():
        S[...] = jnp.zeros_like(S)

    lg = _ret_log_gamma()
    lane = _iota((1, BR), 1)
    head_qk = (lane & 127) >> 5
    head_v = lane >> 6
    lg_lane = _lane_select(head_qk, lg)
    t = _iota((c, 1), 0).astype(F32)
    gq = jnp.exp((t + 1.0) * lg_lane)
    gk = jnp.exp((c - 1.0 - t) * lg_lane)
    rowi = _iota((BR, 1), 0)
    head_row = (rowi & 127) >> 5
    gam = jnp.exp(float(c) * _lane_select(head_row, lg))
    bd = head_row == head_v
    dt = (_iota((c, c), 0) - _iota((c, c), 1))
    dmat = jnp.concatenate(
        [jnp.where(dt >= 0, jnp.exp(dt.astype(F32) * lg[h]), 0.0) for h in range(HEADS)], axis=0)
    avg = _head_avg(BR, RET_DK)
    gn = gn_ref[...]
    for ci in range(nchunk):
        rows = slice(ci * c, (ci + 1) * c)
        q = q_ref[rows, :]
        k = k_ref[rows, :]
        vb = v_ref[rows, :].astype(BF16)
        qs = jnp.concatenate([jnp.where(head_qk == h, q, 0.0) for h in range(HEADS)], axis=0).astype(BF16)
        sc = (_dot_nt(qs, k.astype(BF16)) * dmat).astype(BF16)
        pv = _dot(sc, vb)
        o = _dot((q * gq).astype(BF16), S[...].astype(BF16))
        for h in range(HEADS):
            o = o + jnp.where(head_v == h, pv[h * c:(h + 1) * c, :], 0.0)
        S[...] = gam * S[...] + jnp.where(bd, _dot_tn((k * gk).astype(BF16), vb), 0.0)
        oc = o - _dot2_right(o, avg)
        on = oc * lax.rsqrt(_dot2_right(oc * oc, avg) + EPS)
        gt = g_ref[rows, :]
        o_ref[rows, :] = (on * gn * (gt * _sigmoid(gt))).astype(BF16)

    @pl.when(j == pl.num_programs(1) - 1)
    def _():
        s_ref[0] = S[...]


def _retention(pret, gn, *, batch, seq_len, tc):
    t = pret.shape[0]
    nj = seq_len // tc
    col = lambda cb: pl.BlockSpec((tc, BR), lambda b, j, _cb=cb: (b * nj + j, _cb))
    return pl.pallas_call(
        functools.partial(_ret_kernel, nchunk=tc // RET_CHUNK), name="retention",
        grid=(batch, nj),
        in_specs=[col(0), col(1), col(2), col(3), _full((1, BR))],
        out_specs=[pl.BlockSpec((tc, BR), lambda b, j: (b * nj + j, 0)),
                   pl.BlockSpec((1, BR, BR), lambda b, j: (b, 0, 0))],
        out_shape=[jax.ShapeDtypeStruct((t, BR), BF16), jax.ShapeDtypeStruct((batch, BR, BR), F32)],
        scratch_shapes=[pltpu.VMEM((BR, BR), F32)],
        compiler_params=_cparams(("arbitrary", "arbitrary")),
    )(pret, pret, pret, pret, gn)


def _cmul(ar, ai, br, bi):
    return ar * br - ai * bi, ar * bi + ai * br


def _s5_kernel(u_ref, abre_ref, abim_ref, bb_ref, cc_ref, d_ref, wg_ref, bg_ref,
               y_ref, sre_ref, sim_ref, xr_ref, xi_ref, car_ref):
    j = pl.program_id(1)
    tm = u_ref.shape[0]

    @pl.when(j == 0)
    def _():
        car_ref[...] = jnp.zeros_like(car_ref)

    a1r = abre_ref[...]
    a1i = abim_ref[...]
    u = u_ref[...]
    bu = _dot(u.astype(BF16), bb_ref[...])
    xr = bu[:, :SSM_N]
    xi = bu[:, SSM_N:]
    row8 = _iota((tm, 1), 0) & 7
    pr, pi_ = a1r, a1i
    for k in (1, 2, 4):
        ok = row8 >= k
        sr = jnp.where(ok, pltpu.roll(xr, k, 0), 0.0)
        si = jnp.where(ok, pltpu.roll(xi, k, 0), 0.0)
        dr, di = _cmul(pr, pi_, sr, si)
        xr = xr + dr
        xi = xi + di
        pr, pi_ = _cmul(pr, pi_, pr, pi_)
    r8 = _iota((SUBLANES, 1), 0)
    tr = jnp.broadcast_to(a1r, (SUBLANES, SSM_N))
    ti = jnp.broadcast_to(a1i, (SUBLANES, SSM_N))
    qr, qi = a1r, a1i
    for bit in range(3):
        nr, ni = _cmul(tr, ti, qr, qi)
        use = ((r8 >> bit) & 1) == 1
        tr = jnp.where(use, nr, tr)
        ti = jnp.where(use, ni, ti)
        qr, qi = _cmul(qr, qi, qr, qi)
    xr_ref[...] = xr
    xi_ref[...] = xi

    def body(g, carry):
        cr, ci = carry
        rows = pl.ds(pl.multiple_of(g * SUBLANES, SUBLANES), SUBLANES)
        dr, di = _cmul(tr, ti, cr, ci)
        r2 = xr_ref[rows, :] + dr
        i2 = xi_ref[rows, :] + di
        xr_ref[rows, :] = r2
        xi_ref[rows, :] = i2
        return r2[SUBLANES - 1:SUBLANES, :], i2[SUBLANES - 1:SUBLANES, :]

    cr, ci = lax.fori_loop(0, tm // SUBLANES, body, (car_ref[0:1, :], car_ref[1:2, :]))
    car_ref[0:1, :] = cr
    car_ref[1:2, :] = ci
    sre_ref[0] = cr
    sim_ref[0] = ci
    xcat = jnp.concatenate([xr_ref[...].astype(BF16), xi_ref[...].astype(BF16)], axis=1)
    y = _dot(xcat, cc_ref[...]) + d_ref[...] * u
    yg = _gelu_tanh(y)
    y_ref[...] = (yg * _sigmoid(_dot(yg.astype(BF16), wg_ref[...]) + bg_ref[...])).astype(BF16)


def _s5(su, ab_re, ab_im, bbcat, cccat, d_skip, w_glu, b_glu, layer, *, batch, seq_len, tm):
    t = su.shape[0]
    nj = seq_len // tm
    return pl.pallas_call(
        _s5_kernel, name="s5_scan",
        grid=(batch, nj),
        in_specs=[pl.BlockSpec((tm, BR), lambda b, j: (b * nj + j, 0)), _full((1, SSM_N)), _full((1, SSM_N)),
                  _layer_block((BR, 2 * SSM_N), layer), _layer_block((2 * SSM_N, BR), layer), _full((1, BR)),
                  _layer_block((BR, BR), layer), _full((1, BR))],
        out_specs=[pl.BlockSpec((tm, BR), lambda b, j: (b * nj + j, 0)),
                   pl.BlockSpec((1, 1, SSM_N), lambda b, j: (b, 0, 0)),
                   pl.BlockSpec((1, 1, SSM_N), lambda b, j: (b, 0, 0))],
        out_shape=[jax.ShapeDtypeStruct((t, BR), BF16), jax.ShapeDtypeStruct((batch, 1, SSM_N), F32),
                   jax.ShapeDtypeStruct((batch, 1, SSM_N), F32)],
        scratch_shapes=[pltpu.VMEM((tm, SSM_N), F32), pltpu.VMEM((tm, SSM_N), F32), pltpu.VMEM((SUBLANES, SSM_N), F32)],
        compiler_params=_cparams(("arbitrary", "arbitrary")),
    )(su, ab_re, ab_im, bbcat, cccat, d_skip, w_glu, b_glu)


def _gla_kernel(q_ref, k_ref, la_ref, v_ref, g_ref, gn_ref, o_ref, s_ref, ST, oi_ref, qt_ref, kh_ref, el_ref):
    j = pl.program_id(1)
    tc = q_ref.shape[0]
    c = GLA_CHUNK
    nchunk = tc // c

    @pl.when(j == 0)
    def _():
        ST[...] = jnp.zeros_like(ST)

    r = _iota((tc, tc), 0)
    s = _iota((tc, tc), 1)
    same = (r // c) == (s // c)
    tri = (same & (s <= r)).astype(BF16)
    ones_blk = same.astype(BF16)
    la = la_ref[...]
    hi, mid, lo = _split3(la)
    b = _dot(tri, hi) + _dot(tri, mid) + _dot(tri, lo)
    bl = _dot(ones_blk, hi) + _dot(ones_blk, mid) + _dot(ones_blk, lo)
    q = q_ref[...]
    k = k_ref[...]
    qt = q * jnp.exp(b)
    kt = k * jnp.exp(jnp.minimum(-b, EXP_CLAMP))
    qt_ref[...] = qt
    kh_ref[...] = k * jnp.exp(bl - b)
    el_ref[...] = jnp.exp(bl)
    lane_k = _iota((1, GLA_KW), 1) >> 5
    lane_v = _iota((1, BR), 1) >> 6
    qs = jnp.concatenate([jnp.where(lane_k == h, qt, 0.0) for h in range(HEADS)], axis=0).astype(BF16)
    sc = _dot_nt(qs, kt.astype(BF16))
    keep = jnp.concatenate([same & (s <= r)] * HEADS, axis=0)
    sc = jnp.where(keep, sc, 0.0).astype(BF16)
    vb = v_ref[...].astype(BF16)
    pv = _dot(sc, vb)
    o = jnp.zeros((tc, BR), F32)
    for h in range(HEADS):
        o = o + jnp.where(lane_v == h, pv[h * tc:(h + 1) * tc, :], 0.0)
    oi_ref[...] = o
    bd = (_iota((BR, 1), 0) >> 6) == lane_k

    st = ST[...]
    for ci in range(nchunk):
        rows = slice(ci * c, (ci + 1) * c)
        oi_ref[rows, :] = oi_ref[rows, :] + _dot_nt(qt_ref[rows, :].astype(BF16), st.astype(BF16))
        upd = _dot_tn(v_ref[rows, :].astype(BF16), kh_ref[rows, :].astype(BF16))
        st = el_ref[(ci + 1) * c - 1:(ci + 1) * c, :] * st + jnp.where(bd, upd, 0.0)
    ST[...] = st
    o = oi_ref[...]
    avg = _head_avg(BR, BR // HEADS)
    on = o * lax.rsqrt(_dot2_right(o * o, avg) + EPS)
    gt = g_ref[...]
    o_ref[...] = (on * gn_ref[...] * (gt * _sigmoid(gt))).astype(BF16)

    @pl.when(j == pl.num_programs(1) - 1)
    def _():
        s_ref[0] = ST[...]


def _gla(pgla, gn, *, batch, seq_len, tc):
    t = pgla.shape[0]
    nj = seq_len // tc
    c128 = lambda cb: pl.BlockSpec((tc, 128), lambda b, j, _cb=cb: (b * nj + j, _cb))
    c256 = lambda cb: pl.BlockSpec((tc, BR), lambda b, j, _cb=cb: (b * nj + j, _cb))
    return pl.pallas_call(
        _gla_kernel, name="gla",
        grid=(batch, nj),
        in_specs=[c128(G_Q // 128), c128(G_K // 128), c128(G_LA // 128), c256(G_V // BR), c256(G_G // BR),
                  _full((1, BR))],
        out_specs=[pl.BlockSpec((tc, BR), lambda b, j: (b * nj + j, 0)),
                   pl.BlockSpec((1, BR, GLA_KW), lambda b, j: (b, 0, 0))],
        out_shape=[jax.ShapeDtypeStruct((t, BR), BF16), jax.ShapeDtypeStruct((batch, BR, GLA_KW), F32)],
        scratch_shapes=[pltpu.VMEM((BR, GLA_KW), F32), pltpu.VMEM((tc, BR), F32), pltpu.VMEM((tc, GLA_KW), F32),
                        pltpu.VMEM((tc, GLA_KW), F32), pltpu.VMEM((tc, GLA_KW), F32)],
        compiler_params=_cparams(("arbitrary", "arbitrary")),
    )(pgla, pgla, pgla, pgla, pgla, gn)


NEG = -1e30


def _fox_kernel(qi_tab, kj_tab, q_ref, k_ref, v_ref, o_ref, m_ref, acc_ref):
    p = pl.program_id(1)
    qi = qi_tab[p]
    kj = kj_tab[p]
    tq = q_ref.shape[1]
    ntile = tq // 128

    @pl.when(kj == 0)
    def _():
        m_ref[...] = jnp.full_like(m_ref, NEG)
        acc_ref[...] = jnp.zeros_like(acc_ref)

    def step(diag):
        if diag:
            causal = _iota((tq, tq), 1) <= _iota((tq, tq), 0)
        for h in range(HEADS):
            s = _dot_nt(q_ref[h], k_ref[h])
            if diag:
                s = jnp.where(causal, s, NEG)
            tiles = [s[:, 128 * c:128 * (c + 1)] for c in range(ntile)]
            m4 = functools.reduce(jnp.maximum, tiles)
            m_prev = m_ref[h]
            m_new = jnp.maximum(m_prev, jnp.max(m4, axis=1, keepdims=True))
            alpha = jnp.exp2(m_prev - m_new)
            pe = jnp.concatenate([jnp.exp2(t - m_new).astype(BF16) for t in tiles], axis=1)
            acc_ref[h] = alpha * acc_ref[h] + _dot(pe, v_ref[h])
            m_ref[h] = m_new

    @pl.when(kj != qi)
    def _():
        step(False)

    @pl.when(kj == qi)
    def _():
        step(True)
        low = _iota((tq, 128), 1) < FOX_HD
        for pair in range(HEADS // 2):
            a0 = acc_ref[2 * pair]
            a1 = acc_ref[2 * pair + 1]
            oe = jnp.where(low, a0 / a0[:, FOX_HD:FOX_HD + 1], 0.0)
            oo = jnp.where(low, a1 / a1[:, FOX_HD:FOX_HD + 1], 0.0)
            o_ref[:, 128 * pair:128 * pair + 128] = (oe + pltpu.roll(oo, FOX_HD, 1)).astype(BF16)


def _fox_prompt(fq, fk, fv, *, batch, seq_len, tq):
    t = fq.shape[1]
    nq = seq_len // tq
    pairs = [(i, j) for i in range(nq) for j in range(i + 1)]
    qi_tab = jnp.asarray([p[0] for p in pairs], jnp.int32)
    kj_tab = jnp.asarray([p[1] for p in pairs], jnp.int32)
    grid_spec = pltpu.PrefetchScalarGridSpec(
        num_scalar_prefetch=2, grid=(batch, len(pairs)),
        in_specs=[pl.BlockSpec((HEADS, tq, 128), lambda b, p, qt, kt: (0, b * nq + qt[p], 0)),
                  pl.BlockSpec((HEADS, tq, 128), lambda b, p, qt, kt: (0, b * nq + kt[p], 0)),
                  pl.BlockSpec((HEADS, tq, 128), lambda b, p, qt, kt: (0, b * nq + kt[p], 0))],
        out_specs=pl.BlockSpec((tq, BR), lambda b, p, qt, kt: (b * nq + qt[p], 0)),
        scratch_shapes=[pltpu.VMEM((HEADS, tq, 128), F32)] * 2)
    return pl.pallas_call(
        _fox_kernel, grid_spec=grid_spec, out_shape=jax.ShapeDtypeStruct((t, BR), BF16), name="fox_prompt",
        compiler_params=_cparams(("arbitrary", "arbitrary")),
    )(qi_tab, kj_tab, fq, fk, fv)


def _merge_kernel(x_ref, o0_ref, o1_ref, o2_ref, o3_ref, g_ref, wg_ref, bg_ref, wb_ref, wo_ref, out_ref):
    x = x_ref[...]
    h = _rms(x, g_ref[...]).astype(BF16)
    merged = None
    for n, o_ref in enumerate((o0_ref, o1_ref, o2_ref, o3_ref)):
        cols = slice(n * D_MODEL, (n + 1) * D_MODEL)
        gate = _sigmoid(_dot(h, wg_ref[:, cols]) + bg_ref[:, cols])
        term = gate * _dot(o_ref[...], wb_ref[n])
        merged = term if merged is None else merged + term
    out_ref[...] = x + _dot(merged.astype(BF16), wo_ref[...])


def _merge(x2d, branches, g, wg, bg, wb, wo, layer, *, tm):
    t = x2d.shape[0]
    row = lambda c: pl.BlockSpec((tm, c), lambda i: (i, 0))
    return pl.pallas_call(
        _merge_kernel, grid=(t // tm,), name="merge",
        in_specs=[row(D_MODEL), row(BR), row(BR), row(BR), row(BR), _full((1, D_MODEL)),
                  _layer_block((D_MODEL, N_BRANCH * D_MODEL), layer), _full((1, N_BRANCH * D_MODEL)),
                  _layer_block((N_BRANCH, BR, D_MODEL), layer), _layer_block((D_MODEL, D_MODEL), layer)],
        out_specs=row(D_MODEL), out_shape=jax.ShapeDtypeStruct((t, D_MODEL), F32),
        compiler_params=_cparams(("arbitrary",)),
    )(x2d, *branches, g, wg, bg, wb, wo)


def _ffn_kernel(*refs, prompt, tiles_per_batch, final):
    if prompt:
        x_ref, g_ref, wu_ref, cw_ref, cb_ref, wd_ref, gf_ref, out_ref, tail_ref, prev_ref = refs
    else:
        x_ref, g_ref, wu_ref, cw_ref, cb_ref, wd_ref, gf_ref, c0_ref, c1_ref, out_ref, a_ref = refs
    x = x_ref[...]
    tm = x.shape[0]
    h = _rms(x, g_ref[...]).astype(BF16)
    if prompt:
        i = pl.program_id(0)

        @pl.when(i % tiles_per_batch == 0)
        def _():
            prev_ref[...] = jnp.zeros_like(prev_ref)

        rowid = _iota((tm, 1), 0)
    y = x
    for ch in range(D_FF // FF_CHUNK):
        cols = slice(ch * FF_CHUNK, (ch + 1) * FF_CHUNK)
        gcols = slice(D_FF + ch * FF_CHUNK, D_FF + (ch + 1) * FF_CHUNK)
        a = _dot(h, wu_ref[:, cols])
        gate = _dot(h, wu_ref[:, gcols])
        if prompt:
            p1 = prev_ref[SUBLANES - 1:SUBLANES, cols]
            p2 = prev_ref[SUBLANES - 2:SUBLANES - 1, cols]
            a1 = jnp.where(rowid == 0, p1, pltpu.roll(a, 1, 0))
            a2 = jnp.where(rowid == 0, p2, jnp.where(rowid == 1, p1, pltpu.roll(a, 2, 0)))
            prev_ref[:, cols] = a[tm - SUBLANES:tm, :]
            tail_ref[0, :, cols] = a[tm - SUBLANES:tm, :]
        else:
            a1 = c1_ref[:, cols]
            a2 = c0_ref[:, cols]
            a_ref[:, cols] = a
        conv = cb_ref[:, cols] + cw_ref[0:1, cols] * a2 + cw_ref[1:2, cols] * a1 + cw_ref[2:3, cols] * a
        act = (_gelu_tanh(conv) * gate).astype(BF16)
        y = y + _dot(act, wd_ref[cols, :])
    out_ref[...] = _rms(y, gf_ref[...]) if final else y


def _ffn(x2d, g, wu, cw, cb, wd, gfinal, layer, conv0=None, *, prompt, batch, seq_len, tm):
    t = x2d.shape[0]
    tpb = max(seq_len // tm, 1)
    final = layer == wu.shape[0] - 1
    row = lambda c: pl.BlockSpec((tm, c), lambda i: (i, 0))
    in_specs = [row(D_MODEL), _full((1, D_MODEL)), _layer_block((D_MODEL, 2 * D_FF), layer),
                _layer_block((CONV_W, D_FF), layer), _full((1, D_FF)), _layer_block((D_FF, D_MODEL), layer),
                _full((1, D_MODEL))]
    args = [x2d, g, wu, cw, cb, wd, gfinal]
    if prompt:
        out_shape = [jax.ShapeDtypeStruct((t, D_MODEL), F32), jax.ShapeDtypeStruct((batch, SUBLANES, D_FF), F32)]
        out_specs = [row(D_MODEL), pl.BlockSpec((1, SUBLANES, D_FF), lambda i: (i // tpb, 0, 0))]
        scratch = [pltpu.VMEM((SUBLANES, D_FF), F32)]
    else:
        in_specs += [row(D_FF), row(D_FF)]
        args += [conv0[:, 0, :], conv0[:, 1, :]]
        out_shape = [jax.ShapeDtypeStruct((t, D_MODEL), F32), jax.ShapeDtypeStruct((t, D_FF), F32)]
        out_specs = [row(D_MODEL), row(D_FF)]
        scratch = []
    return pl.pallas_call(
        functools.partial(_ffn_kernel, prompt=prompt, tiles_per_batch=tpb, final=final), name="conv_ffn",
        grid=(t // tm,), in_specs=in_specs, out_specs=out_specs, out_shape=out_shape,
        scratch_shapes=scratch, compiler_params=_cparams(("arbitrary",)),
    )(*args)


def _sample_mix_kernel(rq_ref, rk_ref, rv_ref, rg_ref, rs_ref, rgn_ref,
                       gq_ref, gk_ref, gla_ref, gv_ref, gg_ref, gs_ref, ggn_ref,
                       u_ref, sre_ref, sim_ref, abre_ref, abim_ref, bb_ref, cc_ref, d_ref, wg_ref, bg_ref,
                       oret_ref, rsn_ref, ogla_ref, gsn_ref, ossm_ref, sren_ref, simn_ref):
    lg = _ret_log_gamma()
    v = rv_ref[...]
    for h in range(HEADS):
        sn = math.exp(lg[h]) * rs_ref[:, h] + rk_ref[:, h] * v[:, h]
        rsn_ref[:, h] = sn
        o = jnp.sum(rq_ref[:, h] * sn, axis=1)
        oc = o - jnp.mean(o, axis=-1, keepdims=True)
        on = oc * lax.rsqrt(jnp.mean(oc * oc, axis=-1, keepdims=True) + EPS)
        gt = rg_ref[:, h]
        oret_ref[:, h] = on * rgn_ref[h] * (gt * _sigmoid(gt))
    gv = gv_ref[...]
    for h in range(HEADS):
        sn = jnp.exp(gla_ref[:, h]) * gs_ref[:, h] + gk_ref[:, h] * gv[:, h]
        gsn_ref[:, h] = sn
        o = jnp.sum(gq_ref[:, h] * sn, axis=1)
        on = o * lax.rsqrt(jnp.mean(o * o, axis=-1, keepdims=True) + EPS)
        gt = gg_ref[:, h]
        ogla_ref[:, h] = on * ggn_ref[h] * (gt * _sigmoid(gt))
    u = u_ref[...]
    bu = _dot(u.astype(BF16), bb_ref[...])
    dr, di = _cmul(abre_ref[...], abim_ref[...], sre_ref[...], sim_ref[...])
    xr = bu[:, :SSM_N] + dr
    xi = bu[:, SSM_N:] + di
    sren_ref[...] = xr
    simn_ref[...] = xi
    xcat = jnp.concatenate([xr.astype(BF16), xi.astype(BF16)], axis=1)
    y = _dot(xcat, cc_ref[...]) + d_ref[...] * u
    yg = _gelu_tanh(y)
    ossm_ref[...] = yg * _sigmoid(_dot(yg.astype(BF16), wg_ref[...]) + bg_ref[...])


def _sample_mix(pret, pgla, su, st_ret, st_gla, st_re, st_im, rgn, ggn, ab_re, ab_im, bbcat, cccat, d_skip, w_glu, b_glu):
    n = pret.shape[0]

    def unpack_qk(a):
        return jnp.transpose(a.reshape(n, 2, HEADS, 32), (0, 2, 1, 3)).reshape(n, HEADS, RET_DK, 1)

    rq = unpack_qk(pret[:, 0:256])
    rk = unpack_qk(pret[:, 256:512])
    rv = pret[:, 512:768].reshape(n, HEADS, 1, 64)
    rg = pret[:, 768:1024].reshape(n, HEADS, 64)
    gq = pgla[:, G_Q:G_K].reshape(n, HEADS, GLA_DK, 1)
    gk = pgla[:, G_K:G_LA].reshape(n, HEADS, GLA_DK, 1)
    gla = pgla[:, G_LA:G_LA + 128].reshape(n, HEADS, GLA_DK, 1)
    gv = pgla[:, G_V:G_G].reshape(n, HEADS, 1, 64)
    gg = pgla[:, G_G:G_Q].reshape(n, HEADS, 64)
    out_shape = [jax.ShapeDtypeStruct((n, HEADS, 64), F32), jax.ShapeDtypeStruct(st_ret.shape, F32),
                 jax.ShapeDtypeStruct((n, HEADS, 64), F32), jax.ShapeDtypeStruct(st_gla.shape, F32),
                 jax.ShapeDtypeStruct((n, BR), F32), jax.ShapeDtypeStruct((n, SSM_N), F32),
                 jax.ShapeDtypeStruct((n, SSM_N), F32)]
    outs = pl.pallas_call(
        _sample_mix_kernel, out_shape=out_shape, name="sample_mixers",
        compiler_params=pltpu.CompilerParams(vmem_limit_bytes=VMEM_LIMIT),
    )(rq, rk, rv, rg, st_ret, rgn.reshape(HEADS, 1, 64), gq, gk, gla, gv, gg, st_gla, ggn.reshape(HEADS, 1, 64),
      su, st_re.reshape(n, SSM_N), st_im.reshape(n, SSM_N), ab_re, ab_im, bbcat, cccat, d_skip, w_glu, b_glu)
    o_ret, ret_new, o_gla, gla_new, o_ssm, re_new, im_new = outs
    return (o_ret.reshape(n, BR), o_ssm, o_gla.reshape(n, BR), ret_new, gla_new,
            re_new.reshape(n, SSM_G, SSM_P), im_new.reshape(n, SSM_G, SSM_P))


def _fox_sample_kernel(pt_ref, q_ref, kn_ref, vn_ref, lfn_ref, lfpool_ref, ck_hbm, cv_hbm, o_ref,
                       kbuf, vbuf, sem, qb_ref, m_ref, l_ref, acc_ref, car_ref, *, npp, n_pages, n_samples, layer):
    steps = n_pages // npp
    total = n_samples * steps

    def page_id(g, i):
        b = lax.div(g, steps)
        p = lax.rem(g, steps)
        return pt_ref[b * n_pages + n_pages - 1 - (p * npp + i)]

    def page_copies(g, slot):
        cps = []
        for i in range(npp):
            pid = page_id(g, i)
            cps.append(pltpu.make_async_copy(ck_hbm.at[layer, pid], kbuf.at[slot, i], sem.at[0, slot]))
            cps.append(pltpu.make_async_copy(cv_hbm.at[layer, pid], vbuf.at[slot, i], sem.at[1, slot]))
        return cps

    def head_rows(x):
        out = jnp.zeros((HEADS, x.shape[1]), F32)
        for h in range(HEADS):
            r = jnp.sum(x[FOX_HD * h:FOX_HD * (h + 1), :], axis=0, keepdims=True)
            out = jnp.where(_iota(out.shape, 0) == h, r, out)
        return out

    for cp in page_copies(0, 0):
        cp.start()
    if total > 1:
        for cp in page_copies(1, 1):
            cp.start()

    def body(g, carry):
        slot = g & 1
        b = lax.div(g, steps)
        p = lax.rem(g, steps)
        for cp in page_copies(g, slot):
            cp.wait()

        @pl.when(p == 0)
        def _():
            qcol = q_ref[b]
            qb_ref[...] = jnp.broadcast_to(qcol, qb_ref.shape)
            m_ref[...] = jnp.broadcast_to(head_rows(qcol * kn_ref[b]), m_ref.shape)
            l_ref[...] = jnp.ones_like(l_ref)
            acc_ref[...] = jnp.where(_iota(acc_ref.shape, 1) == 0, vn_ref[b], 0.0)
            car_ref[...] = jnp.broadcast_to(lfn_ref[b], car_ref.shape)

        qb = qb_ref[...]
        car = car_ref[...]
        logits = []
        lf_all = jnp.concatenate([lfpool_ref[page_id(g, i)] for i in range(npp)], axis=0)
        upper = (_iota((PAGE, PAGE), 0) >= _iota((PAGE, PAGE), 1)).astype(BF16)
        suf_all = _dot3_right(lf_all, upper)
        for i in range(npp):
            lf = lf_all[SUBLANES * i:SUBLANES * i + HEADS, :]
            suf = suf_all[SUBLANES * i:SUBLANES * i + HEADS, :]
            s = head_rows(kbuf[slot, i] * qb) + ((suf - lf) + car) * LOG2E
            car = car + suf[:, 0:1]
            logits.append(s)
        m_prev = m_ref[...]
        m_new = jnp.maximum(m_prev, jnp.max(functools.reduce(jnp.maximum, logits), axis=1, keepdims=True))
        alpha = jnp.exp2(m_prev - m_new)
        probs = [jnp.exp2(s - m_new) for s in logits]
        l_ref[...] = alpha * l_ref[...] + jnp.sum(functools.reduce(jnp.add, probs), axis=1, keepdims=True)
        m_ref[...] = m_new
        car_ref[...] = car
        for h in range(HEADS):
            rows = slice(FOX_HD * h, FOX_HD * (h + 1))
            a = alpha[h:h + 1, :] * acc_ref[rows, :]
            for i in range(npp):
                a = a + vbuf[slot, i, rows, :] * probs[i][h:h + 1, :]
            acc_ref[rows, :] = a

        @pl.when(p == steps - 1)
        def _():
            for h in range(HEADS):
                rows = slice(FOX_HD * h, FOX_HD * (h + 1))
                o_ref[b, rows, :] = jnp.sum(acc_ref[rows, :], axis=1, keepdims=True) / l_ref[h:h + 1, 0:1]

        @pl.when(g + 2 < total)
        def _():
            for cp in page_copies(g + 2, slot):
                cp.start()

        return carry

    lax.fori_loop(0, total, body, 0)


def _fox_sample(page_table, fq, kn, vn, lfn, cache_kt, cache_vt, cache_lft, layer):
    n, n_pages = page_table.shape
    npp = PAGES_PER_STEP
    while n_pages % npp:
        npp //= 2
    pt = page_table.reshape(-1)
    n_pool = cache_lft.shape[1]
    col3 = lambda r: pl.BlockSpec((n, r, 1), lambda i, pt_: (0, 0, 0))
    in_specs = [col3(BR), col3(BR), col3(BR), col3(HEADS),
                pl.BlockSpec((None, n_pool, SUBLANES, PAGE), lambda i, pt_: (layer, 0, 0, 0)),
                pl.BlockSpec(memory_space=pl.ANY), pl.BlockSpec(memory_space=pl.ANY)]
    grid_spec = pltpu.PrefetchScalarGridSpec(
        num_scalar_prefetch=1, grid=(1,), in_specs=in_specs, out_specs=col3(BR),
        scratch_shapes=[pltpu.VMEM((2, npp, BR, PAGE), F32), pltpu.VMEM((2, npp, BR, PAGE), F32),
                        pltpu.SemaphoreType.DMA((2, 2)),
                        pltpu.VMEM((BR, PAGE), F32), pltpu.VMEM((HEADS, PAGE), F32), pltpu.VMEM((HEADS, PAGE), F32),
                        pltpu.VMEM((BR, PAGE), F32), pltpu.VMEM((HEADS, PAGE), F32)])
    out = pl.pallas_call(
        functools.partial(_fox_sample_kernel, npp=npp, n_pages=n_pages, n_samples=n, layer=layer),
        grid_spec=grid_spec, out_shape=jax.ShapeDtypeStruct((n, BR, 1), F32), name="fox_sample",
        compiler_params=_cparams(("arbitrary",)),
    )(pt, fq[:, :, None], kn[:, :, None], vn[:, :, None], lfn[:, :, None], cache_lft, cache_kt, cache_vt)
    return out.reshape(n, BR)


def _pack_w_in(w):
    def rope_pack(blk):
        d = blk.shape[0]
        b4 = blk.reshape(d, D_MODEL, HEADS, 2, 32)
        return jnp.transpose(b4, (0, 1, 3, 2, 4)).reshape(d, D_MODEL, 256)

    depth = w.shape[0]
    small = jnp.zeros((depth, D_MODEL, 128), w.dtype)
    small = small.at[:, :, 0:HEADS].set(w[:, :, N_FF:N_FF + HEADS])
    small = small.at[:, :, HEADS:HEADS + GLA_RANK].set(w[:, :, N_GA:N_GA + GLA_RANK])
    parts = [rope_pack(w[:, :, N_RQ:N_RK]), rope_pack(w[:, :, N_RK:N_RV]) * (RET_DK ** -0.5),
             w[:, :, N_RV:N_GQ],
             w[:, :, N_GQ:N_GA],
             w[:, :, N_FQ:N_FK] * FOX_QSCALE, w[:, :, N_FK:N_FF], small]
    return jnp.concatenate(parts, axis=-1).astype(BF16)


def _rope_tables(pos):
    inv = ROPE_BASE ** (-jnp.arange(32, dtype=F32) * 2.0 / RET_DK)
    ang = pos.astype(F32)[:, None] * inv[None, :]
    return jnp.tile(jnp.cos(ang), (1, HEADS)), jnp.tile(jnp.sin(ang), (1, HEADS))


def _fox_consts():
    eall = np.zeros((128, 8 * 128), np.float32)
    for h in range(HEADS):
        for part in range(3):
            eall[part * HEADS + h, 128 * h + FOX_HD + part] = 1.0
            eall[part * HEADS + h, 128 * (HEADS + h) + FOX_HD + 3 + part] = -1.0
    cq = np.zeros((1, 128), np.float32)
    ck = np.zeros((1, 128), np.float32)
    cq[0, FOX_HD + 3:FOX_HD + 6] = 1.0
    ck[0, FOX_HD:FOX_HD + 3] = 1.0
    return jnp.asarray(eall, BF16), jnp.asarray(cq), jnp.asarray(ck)


def _diag_blocks(s, nh, dk, dv):
    b = s.shape[0]
    s5 = s.reshape(b, nh, dk, nh, dv)
    idx = jnp.arange(nh)
    return jnp.transpose(s5[:, idx, :, idx, :], (1, 0, 2, 3))


def kernel(x_prompt, x_sample, cache_k, cache_v, cache_logf, state_ret, state_ssm_re, state_ssm_im, state_gla, state_conv, page_table, norm1_g, w_in, w_mgate, b_mgate, ret_norm_g, ssm_a_re, ssm_a_im, ssm_log_dt, ssm_b_re, ssm_b_im, ssm_c_re, ssm_c_im, ssm_d, ssm_w_glu, ssm_b_glu, gla_w_a2, gla_b_a, gla_norm_g, fox_b_f, w_branch, w_out, norm2_g, ffn_w_up, ffn_conv_w, ffn_conv_b, ffn_w_down, final_norm_g):
    depth = w_in.shape[0]
    B, L, _ = x_prompt.shape
    NS = x_sample.shape[0]
    n_pages = page_table.shape[1]
    past = n_pages * PAGE
    T = B * L
    tm = min(512, L)
    n_pool = cache_k.shape[1]

    w_in_p = _pack_w_in(w_in)
    wa2 = jnp.zeros((depth, 128, 128), F32).at[:, HEADS:HEADS + GLA_RANK, :].set(gla_w_a2).astype(BF16)
    bfrow = jnp.zeros((depth, 1, 128), F32).at[:, 0, 0:HEADS].set(fox_b_f)
    wg_b = w_mgate.astype(BF16)
    wb_b = w_branch.astype(BF16)
    wo_b = w_out.astype(BF16)
    wu_b = ffn_w_up.astype(BF16)
    wd_b = ffn_w_down.astype(BF16)
    wglu_b = ssm_w_glu.astype(BF16)
    ab_re, ab_im, bbcat = _s5_prep(ssm_a_re, ssm_a_im, ssm_log_dt, ssm_b_re, ssm_b_im)
    cccat = _s5_cmat(ssm_c_re, ssm_c_im)
    cos_p, sin_p = _rope_tables(jnp.arange(L))
    cos_s, sin_s = _rope_tables(jnp.full((NS,), past))
    fox_consts = _fox_consts()
    ckt = jnp.transpose(cache_k, (0, 1, 3, 4, 2)).reshape(depth, n_pool, BR, PAGE)
    cvt = jnp.transpose(cache_v, (0, 1, 3, 4, 2)).reshape(depth, n_pool, BR, PAGE)
    clft = jnp.pad(jnp.transpose(cache_logf, (0, 1, 3, 2)), ((0, 0), (0, 0), (0, SUBLANES - HEADS), (0, 0)))

    row = lambda a, l: a[l][None, :]
    xp = x_prompt.reshape(T, D_MODEL)
    xs = x_sample.reshape(NS, D_MODEL)
    outs_p, outs_s = [], []
    stacked = (jnp.zeros((depth, B, BR, L), F32), jnp.zeros((depth, B, BR, L), F32),
               jnp.zeros((depth, B, HEADS, L), F32))
    for l in range(depth):
        (pret, su, pgla, kt_all, vt_all, lft_all, fq, fk, fv) = _inproj(
            xp, row(norm1_g, l), w_in_p, wa2, row(gla_b_a, l), bfrow[l], cos_p, sin_p, fox_consts, l, stacked,
            prompt=True, seq_len=L, tm=tm)
        stacked = (kt_all, vt_all, lft_all)
        o_ret, s_ret = _retention(pret, row(ret_norm_g, l), batch=B, seq_len=L, tc=tm)
        o_ssm, s_re, s_im = _s5(su, ab_re[l], ab_im[l], bbcat, cccat, row(ssm_d, l), wglu_b,
                                row(ssm_b_glu, l), l, batch=B, seq_len=L, tm=tm)
        o_gla, s_gla = _gla(pgla, row(gla_norm_g, l), batch=B, seq_len=L, tc=tm)
        o_fox = _fox_prompt(fq, fk, fv, batch=B, seq_len=L, tq=min(FOX_TQ, L))
        xp = _merge(xp, (o_ret, o_ssm, o_gla, o_fox), row(norm1_g, l), wg_b, row(b_mgate, l), wb_b, wo_b, l, tm=tm)
        xp, tail = _ffn(xp, row(norm2_g, l), wu_b, ffn_conv_w, row(ffn_conv_b, l), wd_b, final_norm_g[None, :], l,
                        prompt=True, batch=B, seq_len=L, tm=tm)
        sr = jnp.transpose(s_ret.reshape(B, 2, HEADS, 32, BR), (0, 2, 1, 3, 4)).reshape(B, BR, BR)
        outs_p.append((_diag_blocks(sr, HEADS, RET_DK, 64), s_re.reshape(B, SSM_G, SSM_P), s_im.reshape(B, SSM_G, SSM_P),
                       _diag_blocks(jnp.transpose(s_gla, (0, 2, 1)), HEADS, GLA_DK, 64),
                       tail[:, SUBLANES - (CONV_W - 1):, :]))
        (pret, su, pgla, knat, vnat, lf, fqn) = _inproj(
            xs, row(norm1_g, l), w_in_p, wa2, row(gla_b_a, l), bfrow[l], cos_s, sin_s, None, l,
            prompt=False, seq_len=NS, tm=NS)
        (o_ret, o_ssm, o_gla, ret_new, gla_new, re_new, im_new) = _sample_mix(
            pret, pgla, su, state_ret[l], state_gla[l], state_ssm_re[l], state_ssm_im[l],
            ret_norm_g[l], gla_norm_g[l], ab_re[l], ab_im[l], bbcat[l], cccat[l], row(ssm_d, l), wglu_b[l],
            row(ssm_b_glu, l))
        lfs = lf[:, :HEADS]
        o_fox = _fox_sample(page_table, fqn, knat, vnat, lfs, ckt, cvt, clft, l)
        xs = _merge(xs, (o_ret.astype(BF16), o_ssm.astype(BF16), o_gla.astype(BF16), o_fox.astype(BF16)),
                    row(norm1_g, l), wg_b, row(b_mgate, l), wb_b, wo_b, l, tm=NS)
        xs, a_s = _ffn(xs, row(norm2_g, l), wu_b, ffn_conv_w, row(ffn_conv_b, l), wd_b, final_norm_g[None, :], l,
                       state_conv[l], prompt=False, batch=NS, seq_len=NS, tm=NS)
        new_conv = jnp.stack([state_conv[l][:, 1, :], a_s], axis=1)
        outs_s.append((ret_new, re_new, im_new, gla_new, new_conv,
                       knat.reshape(NS, 1, HEADS, FOX_HD), vnat.reshape(NS, 1, HEADS, FOX_HD), lfs.reshape(NS, 1, HEADS)))

    y_prompt = xp.reshape(B, L, D_MODEL)
    y_sample = xs.reshape(NS, 1, D_MODEL)

    def stk(outs, i):
        return jnp.stack([o[i] for o in outs], axis=0)

    kt_all, vt_all, lft_all = stacked
    k_prompt = jnp.transpose(kt_all.reshape(depth, B, HEADS, FOX_HD, L), (0, 1, 4, 2, 3))
    v_prompt = jnp.transpose(vt_all.reshape(depth, B, HEADS, FOX_HD, L), (0, 1, 4, 2, 3))
    logf_prompt = jnp.transpose(lft_all, (0, 1, 3, 2))
    return (y_prompt, y_sample,
            k_prompt, v_prompt, logf_prompt,
            stk(outs_s, 5), stk(outs_s, 6), stk(outs_s, 7),
            stk(outs_p, 0), stk(outs_s, 0),
            stk(outs_p, 1), stk(outs_p, 2), stk(outs_s, 1), stk(outs_s, 2),
            stk(outs_p, 3), stk(outs_s, 3),
            stk(outs_p, 4), stk(outs_s, 4))
```

```python
import functools
import math

import jax
import jax.numpy as jnp
import numpy as np
from jax import lax
from jax.experimental import pallas as pl
from jax.experimental.pallas import tpu as pltpu

F32 = jnp.float32
BF16 = jnp.bfloat16

LANES = 128
SUBLANES = 8
VMEM_BYTES_V7X = 64 * 1024 * 1024
VMEM_LIMIT = VMEM_BYTES_V7X - 8 * 1024 * 1024

D_MODEL = 1024
BR = 256
N_BRANCH = 4
HEADS = 4
RET_DK = 64
RET_CHUNK = 128
SSM_G = 16
SSM_GC = 16
SSM_P = 64
SSM_N = SSM_G * SSM_P
GLA_DK = 32
GLA_KW = HEADS * GLA_DK
GLA_RANK = 16
GLA_TAU = 16.0
GLA_CHUNK = 32
GLA_SUBBLOCK = 128
FOX_HD = 64
D_FF = 2816
FF_CHUNK = 1408
CONV_W = 3
EPS = 1e-6
ROPE_BASE = 10000.0
PAGE = 128
PAGES_PER_STEP = 8
PAGE_SLOTS = 4
EXP_CLAMP = 80.0
LOG2E = math.log2(math.e)
FOX_QSCALE = (FOX_HD ** -0.5) * LOG2E
FOX_TQ = 1024

C_RQ, C_RK, C_RV, C_RG, C_SU = 0, 256, 512, 768, 1024
C_GQ, C_GK, C_GV, C_GG = 1280, 1408, 1536, 1792
C_FQ, C_FK, C_FV, C_SM = 2048, 2304, 2560, 2816
W_IN_COLS = 2944
G_V, G_G, G_Q, G_K, G_LA = 0, 256, 512, 640, 768
N_RQ, N_RK, N_RV, N_RG, N_SU = 0, 256, 512, 768, 1024
N_GQ, N_GK, N_GV, N_GG, N_GA = 1280, 1408, 1536, 1792, 2048
N_FQ, N_FK, N_FV, N_FF = 2064, 2320, 2576, 2832


def _cparams(sem):
    return pltpu.CompilerParams(dimension_semantics=sem, vmem_limit_bytes=VMEM_LIMIT)


def _dot(a, b):
    return jnp.dot(a, b, preferred_element_type=F32)


def _dot_nt(a, b):
    return lax.dot_general(a, b, (((1,), (1,)), ((), ())), preferred_element_type=F32)


def _dot_tn(a, b):
    return lax.dot_general(a, b, (((0,), (0,)), ((), ())), preferred_element_type=F32)


def _split3(x):
    hi = x.astype(BF16)
    r = x - hi.astype(F32)
    mid = r.astype(BF16)
    lo = (r - mid.astype(F32)).astype(BF16)
    return hi, mid, lo


def _dot3_left(m, x):
    hi, mid, lo = _split3(x)
    return _dot(m, hi) + _dot(m, mid) + _dot(m, lo)


def _dot3_right(x, m):
    hi, mid, lo = _split3(x)
    return _dot(hi, m) + _dot(mid, m) + _dot(lo, m)


def _dot2_right(x, m):
    hi = x.astype(BF16)
    lo = (x - hi.astype(F32)).astype(BF16)
    return _dot(hi, m) + _dot(lo, m)


def _sigmoid(x):
    return 1.0 / (1.0 + jnp.exp(-x))


def _log_sigmoid(x):
    return jnp.minimum(x, 0.0) - jnp.log(1.0 + jnp.exp(-jnp.abs(x)))


def _gelu_tanh(x):
    return 0.5 * x * (1.0 + jnp.tanh(math.sqrt(2.0 / math.pi) * (x + 0.044715 * (x * x * x))))


def _rms(x, g):
    return x * lax.rsqrt(jnp.mean(x * x, axis=-1, keepdims=True) + EPS) * g


def _iota(shape, dim):
    return lax.broadcasted_iota(jnp.int32, shape, dim)


def _head_avg(width, hd):
    r = _iota((width, width), 0) // hd
    c = _iota((width, width), 1) // hd
    return jnp.where(r == c, 1.0 / hd, 0.0).astype(BF16)


def _full(shape):
    nd = len(shape)
    return pl.BlockSpec(shape, lambda *a, _nd=nd: (0,) * _nd)


def _s5_prep_kernel(are_ref, aim_ref, ldt_ref, bre_ref, bim_ref, abre_ref, abim_ref, bbre_ref, bbim_ref):
    a_re = are_ref[...]
    a_im = aim_ref[...]
    dt = jnp.exp(ldt_ref[...])
    mag = jnp.exp(a_re * dt)
    ab_re = mag * jnp.cos(a_im * dt)
    ab_im = mag * jnp.sin(a_im * dt)
    den = a_re * a_re + a_im * a_im
    f_re = ((ab_re - 1.0) * a_re + ab_im * a_im) / den
    f_im = (ab_im * a_re - (ab_re - 1.0) * a_im) / den
    abre_ref[...] = ab_re
    abim_ref[...] = ab_im
    b_re = bre_ref[...]
    b_im = bim_ref[...]
    bbre_ref[...] = f_re[:, None, :] * b_re - f_im[:, None, :] * b_im
    bbim_ref[...] = f_re[:, None, :] * b_im + f_im[:, None, :] * b_re


def _s5_prep(a_re, a_im, log_dt, b_re, b_im):
    depth = a_re.shape[0]
    r = depth * SSM_G
    are = a_re.reshape(r, SSM_P)
    aim = a_im.reshape(r, SSM_P)
    ldt = jnp.broadcast_to(log_dt.reshape(r, 1), (r, SSM_P))
    bre = jnp.transpose(b_re, (0, 1, 3, 2)).reshape(r, SSM_GC, SSM_P)
    bim = jnp.transpose(b_im, (0, 1, 3, 2)).reshape(r, SSM_GC, SSM_P)
    outs = pl.pallas_call(
        _s5_prep_kernel,
        out_shape=(jax.ShapeDtypeStruct((r, SSM_P), F32), jax.ShapeDtypeStruct((r, SSM_P), F32),
                   jax.ShapeDtypeStruct((r, SSM_GC, SSM_P), F32), jax.ShapeDtypeStruct((r, SSM_GC, SSM_P), F32)),
    )(are, aim, ldt, bre, bim)
    ab_re, ab_im, bb_re, bb_im = outs
    ab_re = ab_re.reshape(depth, 1, SSM_N)
    ab_im = ab_im.reshape(depth, 1, SSM_N)
    eye = jnp.eye(SSM_G, dtype=F32)

    def blockdiag(bb):
        bb = bb.reshape(depth, SSM_G, SSM_GC, SSM_P)
        return (bb[:, :, :, None, :] * eye[None, :, None, :, None]).reshape(depth, BR, SSM_N)

    bbcat = jnp.concatenate([blockdiag(bb_re), blockdiag(bb_im)], axis=-1).astype(BF16)
    return ab_re, ab_im, bbcat


def _s5_cmat(c_re, c_im):
    depth = c_re.shape[0]
    eye = jnp.eye(SSM_G, dtype=F32)

    def blockdiag(c):
        ct = jnp.transpose(c, (0, 1, 3, 2))
        return (ct[:, :, :, None, :] * eye[None, :, None, :, None]).reshape(depth, SSM_N, BR)

    return jnp.concatenate([blockdiag(c_re), -blockdiag(c_im)], axis=1).astype(BF16)


def _inproj_kernel(*refs, prompt, tiles_per_batch, n_alias):
    if prompt:
        (x_ref, g_ref, w_ref, wa2_ref, ba_ref, bf_ref, cos_ref, sin_ref, eall_ref, cq_ref, ck_ref) = refs[:11]
        (pret_ref, su_ref, pgla_ref, kt_ref, vt_ref, lft_ref, fq_ref, fk_ref, fv_ref, carry_ref) = refs[11 + n_alias:]
    else:
        (x_ref, g_ref, w_ref, wa2_ref, ba_ref, bf_ref, cos_ref, sin_ref,
         pret_ref, su_ref, pgla_ref, knat_ref, vnat_ref, lf_ref, fqn_ref) = refs
    x = x_ref[...]
    tm = x.shape[0]
    h = _rms(x, g_ref[...]).astype(BF16)
    proj = _dot(h, w_ref[...])
    cos = cos_ref[...]
    sin = sin_ref[...]
    for base in (C_RQ, C_RK):
        x1 = proj[:, base:base + 128]
        x2 = proj[:, base + 128:base + 256]
        pret_ref[:, base:base + 128] = x1 * cos - x2 * sin
        pret_ref[:, base + 128:base + 256] = x1 * sin + x2 * cos
    pret_ref[:, C_RV:C_SU] = proj[:, C_RV:C_SU]
    su_ref[...] = proj[:, C_SU:C_GQ]
    small = proj[:, C_SM:C_SM + 128]
    la = _log_sigmoid(_dot(small.astype(BF16), wa2_ref[...]) + ba_ref[...]) * (1.0 / GLA_TAU)
    pgla_ref[:, G_V:G_Q] = proj[:, C_GV:C_FQ]
    pgla_ref[:, G_Q:G_K] = proj[:, C_GQ:C_GK]
    pgla_ref[:, G_K:G_LA] = proj[:, C_GK:C_GV] * (GLA_DK ** -0.5)
    pgla_ref[:, G_LA:G_LA + 128] = la
    lane = _iota((tm, 128), 1)
    lf = jnp.where(lane < HEADS, _log_sigmoid(small + bf_ref[...]), 0.0)
    if not prompt:
        lf_ref[...] = lf
        knat_ref[...] = proj[:, C_FK:C_FV]
        vnat_ref[...] = proj[:, C_FV:C_SM]
        fqn_ref[...] = proj[:, C_FQ:C_FK]
        return
    kt_ref[...] = proj[:, C_FK:C_FV].T
    vt_ref[...] = proj[:, C_FV:C_SM].T
    lft_ref[...] = lf.T[0:HEADS, :]
    i = pl.program_id(0)

    @pl.when(i % tiles_per_batch == 0)
    def _():
        carry_ref[...] = jnp.zeros_like(carry_ref)

    tri = (_iota((tm, tm), 0) >= _iota((tm, tm), 1)).astype(BF16)
    c = _dot3_left(tri, lf) + carry_ref[0:1, :]
    carry_ref[...] = jnp.broadcast_to(c[tm - 1:tm, :], carry_ref.shape)
    chi, cmid, clo = _split3(c * LOG2E)
    c3 = (chi.astype(F32) + pltpu.roll(cmid.astype(F32), HEADS, 1)
          + pltpu.roll(clo.astype(F32), 2 * HEADS, 1)).astype(BF16)
    aug = _dot(c3, eall_ref[...])
    cq = cq_ref[...]
    ck = ck_ref[...]
    keep = lane < FOX_HD
    for hd in range(HEADS):
        lo = 128 * (hd // 2)
        bq = proj[:, C_FQ + lo:C_FQ + lo + 128]
        bk = proj[:, C_FK + lo:C_FK + lo + 128]
        bv = proj[:, C_FV + lo:C_FV + lo + 128]
        if hd % 2:
            bq = pltpu.roll(bq, FOX_HD, 1)
            bk = pltpu.roll(bk, FOX_HD, 1)
            bv = pltpu.roll(bv, FOX_HD, 1)
        fq_ref[hd] = (jnp.where(keep, bq, 0.0) + aug[:, 128 * hd:128 * hd + 128] + cq).astype(BF16)
        fk_ref[hd] = (jnp.where(keep, bk, 0.0) + aug[:, 128 * (HEADS + hd):128 * (HEADS + hd) + 128] + ck).astype(BF16)
        fv_ref[hd] = jnp.where(keep, bv, jnp.where(lane == FOX_HD, 1.0, 0.0)).astype(BF16)


def _layer_block(shape, layer):
    nd = len(shape)
    return pl.BlockSpec((None,) + tuple(shape), lambda *a, _l=layer, _nd=nd: (_l,) + (0,) * _nd)


def _inproj(x2d, g, w, wa2, ba, bfrow, cos, sin, consts, layer, stacked=None, *, prompt, seq_len, tm):
    t = x2d.shape[0]
    nt = t // tm
    tpb = max(seq_len // tm, 1)
    depth = w.shape[0]
    row = lambda c: pl.BlockSpec((tm, c), lambda i: (i, 0))
    in_specs = [row(D_MODEL), _full((1, D_MODEL)), _layer_block((D_MODEL, W_IN_COLS), layer),
                _layer_block((128, 128), layer), _full((1, 128)), _full((1, 128)),
                pl.BlockSpec((tm, 128), lambda i: (i % tpb, 0)), pl.BlockSpec((tm, 128), lambda i: (i % tpb, 0))]
    args = [x2d, g, w, wa2, ba, bfrow, cos, sin]
    out_shape = [jax.ShapeDtypeStruct((t, 1024), F32), jax.ShapeDtypeStruct((t, BR), F32),
                 jax.ShapeDtypeStruct((t, 896), F32)]
    out_specs = [row(1024), row(BR), row(896)]
    scratch = []
    aliases = {}
    n_alias = 0
    if prompt:
        eall, cq, ck = consts
        batch = t // seq_len
        in_specs += [_full((128, 8 * 128)), _full((1, 128)), _full((1, 128))]
        args += [eall, cq, ck]
        if stacked is not None:
            n_alias = len(stacked)
            aliases = {len(args) + k: len(out_shape) + k for k in range(n_alias)}
            in_specs += [pl.BlockSpec(memory_space=pl.ANY)] * n_alias
            args += list(stacked)
        tok = lambda r: pl.BlockSpec((None, None, r, tm), lambda i, _l=layer: (_l, i // tpb, 0, i % tpb))
        out_shape += [jax.ShapeDtypeStruct((depth, batch, BR, seq_len), F32)] * 2
        out_shape += [jax.ShapeDtypeStruct((depth, batch, HEADS, seq_len), F32)]
        out_specs += [tok(BR), tok(BR), tok(HEADS)]
        hm = pl.BlockSpec((HEADS, tm, 128), lambda i: (0, i, 0))
        out_shape += [jax.ShapeDtypeStruct((HEADS, t, 128), BF16)] * 3
        out_specs += [hm, hm, hm]
        scratch = [pltpu.VMEM((SUBLANES, 128), F32)]
    else:
        out_shape += [jax.ShapeDtypeStruct((t, BR), F32), jax.ShapeDtypeStruct((t, BR), F32),
                      jax.ShapeDtypeStruct((t, 128), F32), jax.ShapeDtypeStruct((t, BR), F32)]
        out_specs += [row(BR), row(BR), row(128), row(BR)]
    return pl.pallas_call(
        functools.partial(_inproj_kernel, prompt=prompt, tiles_per_batch=tpb, n_alias=n_alias),
        grid=(nt,), in_specs=in_specs, out_specs=out_specs, out_shape=out_shape,
        scratch_shapes=scratch, input_output_aliases=aliases, compiler_params=_cparams(("arbitrary",)),
        name="inproj_prompt" if prompt else "inproj_sample",
    )(*args)


def _ret_log_gamma():
    return [math.log(1.0 - 2.0 ** (-5.0 - h)) for h in range(HEADS)]


def _lane_select(idx, values):
    out = jnp.zeros(idx.shape, F32)
    for h, v in enumerate(values):
        out = jnp.where(idx == h, v, out)
    return out


def _ret_kernel(q_ref, k_ref, v_ref, g_ref, gn_ref, o_ref, s_ref, S, *, nchunk):
    j = pl.program_id(1)
    c = RET_CHUNK

    @pl.when(j == 0)
    def _():
        S[...] = jnp.zeros_like(S)

    lg = _ret_log_gamma()
    lane = _iota((1, BR), 1)
    head_qk = (lane & 127) >> 5
    head_v = lane >> 6
    lg_lane = _lane_select(head_qk, lg)
    t = _iota((c, 1), 0).astype(F32)
    gq = jnp.exp((t + 1.0) * lg_lane)
    gk = jnp.exp((c - 1.0 - t) * lg_lane)
    rowi = _iota((BR, 1), 0)
    head_row = (rowi & 127) >> 5
    gam = jnp.exp(float(c) * _lane_select(head_row, lg))
    bd = head_row == head_v
    dt = (_iota((c, c), 0) - _iota((c, c), 1))
    dmat = jnp.concatenate(
        [jnp.where(dt >= 0, jnp.exp(dt.astype(F32) * lg[h]), 0.0) for h in range(HEADS)], axis=0)
    avg = _head_avg(BR, RET_DK)
    gn = gn_ref[...]
    for ci in range(nchunk):
        rows = slice(ci * c, (ci + 1) * c)
        q = q_ref[rows, :]
        k = k_ref[rows, :]
        vb = v_ref[rows, :].astype(BF16)
        qs = jnp.concatenate([jnp.where(head_qk == h, q, 0.0) for h in range(HEADS)], axis=0).astype(BF16)
        sc = (_dot_nt(qs, k.astype(BF16)) * dmat).astype(BF16)
        pv = _dot(sc, vb)
        o = _dot((q * gq).astype(BF16), S[...].astype(BF16))
        for h in range(HEADS):
            o = o + jnp.where(head_v == h, pv[h * c:(h + 1) * c, :], 0.0)
        S[...] = gam * S[...] + jnp.where(bd, _dot_tn((k * gk).astype(BF16), vb), 0.0)
        oc = o - _dot2_right(o, avg)
        on = oc * lax.rsqrt(_dot2_right(oc * oc, avg) + EPS)
        gt = g_ref[rows, :]
        o_ref[rows, :] = (on * gn * (gt * _sigmoid(gt))).astype(BF16)

    @pl.when(j == pl.num_programs(1) - 1)
    def _():
        s_ref[0] = S[...]


def _retention(pret, gn, *, batch, seq_len, tc):
    t = pret.shape[0]
    nj = seq_len // tc
    col = lambda cb: pl.BlockSpec((tc, BR), lambda b, j, _cb=cb: (b * nj + j, _cb))
    return pl.pallas_call(
        functools.partial(_ret_kernel, nchunk=tc // RET_CHUNK), name="retention",
        grid=(batch, nj),
        in_specs=[col(0), col(1), col(2), col(3), _full((1, BR))],
        out_specs=[pl.BlockSpec((tc, BR), lambda b, j: (b * nj + j, 0)),
                   pl.BlockSpec((1, BR, BR), lambda b, j: (b, 0, 0))],
        out_shape=[jax.ShapeDtypeStruct((t, BR), BF16), jax.ShapeDtypeStruct((batch, BR, BR), F32)],
        scratch_shapes=[pltpu.VMEM((BR, BR), F32)],
        compiler_params=_cparams(("arbitrary", "arbitrary")),
    )(pret, pret, pret, pret, gn)


def _cmul(ar, ai, br, bi):
    return ar * br - ai * bi, ar * bi + ai * br


def _s5_kernel(u_ref, abre_ref, abim_ref, bb_ref, cc_ref, d_ref, wg_ref, bg_ref,
               y_ref, sre_ref, sim_ref, xr_ref, xi_ref, car_ref):
    j = pl.program_id(1)
    tm = u_ref.shape[0]

    @pl.when(j == 0)
    def _():
        car_ref[...] = jnp.zeros_like(car_ref)

    a1r = abre_ref[...]
    a1i = abim_ref[...]
    u = u_ref[...]
    bu = _dot(u.astype(BF16), bb_ref[...])
    xr = bu[:, :SSM_N]
    xi = bu[:, SSM_N:]
    row8 = _iota((tm, 1), 0) & 7
    pr, pi_ = a1r, a1i
    for k in (1, 2, 4):
        ok = row8 >= k
        sr = jnp.where(ok, pltpu.roll(xr, k, 0), 0.0)
        si = jnp.where(ok, pltpu.roll(xi, k, 0), 0.0)
        dr, di = _cmul(pr, pi_, sr, si)
        xr = xr + dr
        xi = xi + di
        pr, pi_ = _cmul(pr, pi_, pr, pi_)
    r8 = _iota((SUBLANES, 1), 0)
    tr = jnp.broadcast_to(a1r, (SUBLANES, SSM_N))
    ti = jnp.broadcast_to(a1i, (SUBLANES, SSM_N))
    qr, qi = a1r, a1i
    for bit in range(3):
        nr, ni = _cmul(tr, ti, qr, qi)
        use = ((r8 >> bit) & 1) == 1
        tr = jnp.where(use, nr, tr)
        ti = jnp.where(use, ni, ti)
        qr, qi = _cmul(qr, qi, qr, qi)
    xr_ref[...] = xr
    xi_ref[...] = xi

    def body(g, carry):
        cr, ci = carry
        rows = pl.ds(pl.multiple_of(g * SUBLANES, SUBLANES), SUBLANES)
        dr, di = _cmul(tr, ti, cr, ci)
        r2 = xr_ref[rows, :] + dr
        i2 = xi_ref[rows, :] + di
        xr_ref[rows, :] = r2
        xi_ref[rows, :] = i2
        return r2[SUBLANES - 1:SUBLANES, :], i2[SUBLANES - 1:SUBLANES, :]

    cr, ci = lax.fori_loop(0, tm // SUBLANES, body, (car_ref[0:1, :], car_ref[1:2, :]))
    car_ref[0:1, :] = cr
    car_ref[1:2, :] = ci
    sre_ref[0] = cr
    sim_ref[0] = ci
    xcat = jnp.concatenate([xr_ref[...].astype(BF16), xi_ref[...].astype(BF16)], axis=1)
    y = _dot(xcat, cc_ref[...]) + d_ref[...] * u
    yg = _gelu_tanh(y)
    y_ref[...] = (yg * _sigmoid(_dot(yg.astype(BF16), wg_ref[...]) + bg_ref[...])).astype(BF16)


def _s5(su, ab_re, ab_im, bbcat, cccat, d_skip, w_glu, b_glu, layer, *, batch, seq_len, tm):
    t = su.shape[0]
    nj = seq_len // tm
    return pl.pallas_call(
        _s5_kernel, name="s5_scan",
        grid=(batch, nj),
        in_specs=[pl.BlockSpec((tm, BR), lambda b, j: (b * nj + j, 0)), _full((1, SSM_N)), _full((1, SSM_N)),
                  _layer_block((BR, 2 * SSM_N), layer), _layer_block((2 * SSM_N, BR), layer), _full((1, BR)),
                  _layer_block((BR, BR), layer), _full((1, BR))],
        out_specs=[pl.BlockSpec((tm, BR), lambda b, j: (b * nj + j, 0)),
                   pl.BlockSpec((1, 1, SSM_N), lambda b, j: (b, 0, 0)),
                   pl.BlockSpec((1, 1, SSM_N), lambda b, j: (b, 0, 0))],
        out_shape=[jax.ShapeDtypeStruct((t, BR), BF16), jax.ShapeDtypeStruct((batch, 1, SSM_N), F32),
                   jax.ShapeDtypeStruct((batch, 1, SSM_N), F32)],
        scratch_shapes=[pltpu.VMEM((tm, SSM_N), F32), pltpu.VMEM((tm, SSM_N), F32), pltpu.VMEM((SUBLANES, SSM_N), F32)],
        compiler_params=_cparams(("arbitrary", "arbitrary")),
    )(su, ab_re, ab_im, bbcat, cccat, d_skip, w_glu, b_glu)


def _gla_kernel(q_ref, k_ref, la_ref, v_ref, g_ref, gn_ref, o_ref, s_ref, ST, oi_ref, qt_ref, kh_ref, el_ref):
    j = pl.program_id(1)
    tc = q_ref.shape[0]
    c = GLA_CHUNK
    nchunk = tc // c

    @pl.when(j == 0)
    def _():
        ST[...] = jnp.zeros_like(ST)

    sb = min(tc, GLA_SUBBLOCK)
    r = _iota((sb, sb), 0)
    s = _iota((sb, sb), 1)
    same = (r // c) == (s // c)
    causal = same & (s <= r)
    tri = causal.astype(BF16)
    ones_blk = same.astype(BF16)
    keep = jnp.concatenate([causal] * HEADS, axis=0)
    lane_k = _iota((1, GLA_KW), 1) >> 5
    lane_v = _iota((1, BR), 1) >> 6
    for blk in range(tc // sb):
        rows = slice(blk * sb, (blk + 1) * sb)
        hi, mid, lo = _split3(la_ref[rows, :])
        b = _dot(tri, hi) + _dot(tri, mid) + _dot(tri, lo)
        bl = _dot(ones_blk, hi) + _dot(ones_blk, mid) + _dot(ones_blk, lo)
        k = k_ref[rows, :]
        qt = q_ref[rows, :] * jnp.exp(b)
        kt = k * jnp.exp(jnp.minimum(-b, EXP_CLAMP))
        qt_ref[rows, :] = qt
        kh_ref[rows, :] = k * jnp.exp(bl - b)
        el_ref[rows, :] = jnp.exp(bl)
        qs = jnp.concatenate([jnp.where(lane_k == h, qt, 0.0) for h in range(HEADS)], axis=0).astype(BF16)
        sc = jnp.where(keep, _dot_nt(qs, kt.astype(BF16)), 0.0).astype(BF16)
        pv = _dot(sc, v_ref[rows, :].astype(BF16))
        o = jnp.zeros((sb, BR), F32)
        for h in range(HEADS):
            o = o + jnp.where(lane_v == h, pv[h * sb:(h + 1) * sb, :], 0.0)
        oi_ref[rows, :] = o
    bd = (_iota((BR, 1), 0) >> 6) == lane_k

    st = ST[...]
    for ci in range(nchunk):
        rows = slice(ci * c, (ci + 1) * c)
        oi_ref[rows, :] = oi_ref[rows, :] + _dot_nt(qt_ref[rows, :].astype(BF16), st.astype(BF16))
        upd = _dot_tn(v_ref[rows, :].astype(BF16), kh_ref[rows, :].astype(BF16))
        st = el_ref[(ci + 1) * c - 1:(ci + 1) * c, :] * st + jnp.where(bd, upd, 0.0)
    ST[...] = st
    o = oi_ref[...]
    avg = _head_avg(BR, BR // HEADS)
    on = o * lax.rsqrt(_dot2_right(o * o, avg) + EPS)
    gt = g_ref[...]
    o_ref[...] = (on * gn_ref[...] * (gt * _sigmoid(gt))).astype(BF16)

    @pl.when(j == pl.num_programs(1) - 1)
    def _():
        s_ref[0] = ST[...]


def _gla(pgla, gn, *, batch, seq_len, tc):
    t = pgla.shape[0]
    nj = seq_len // tc
    c128 = lambda cb: pl.BlockSpec((tc, 128), lambda b, j, _cb=cb: (b * nj + j, _cb))
    c256 = lambda cb: pl.BlockSpec((tc, BR), lambda b, j, _cb=cb: (b * nj + j, _cb))
    return pl.pallas_call(
        _gla_kernel, name="gla",
        grid=(batch, nj),
        in_specs=[c128(G_Q // 128), c128(G_K // 128), c128(G_LA // 128), c256(G_V // BR), c256(G_G // BR),
                  _full((1, BR))],
        out_specs=[pl.BlockSpec((tc, BR), lambda b, j: (b * nj + j, 0)),
                   pl.BlockSpec((1, BR, GLA_KW), lambda b, j: (b, 0, 0))],
        out_shape=[jax.ShapeDtypeStruct((t, BR), BF16), jax.ShapeDtypeStruct((batch, BR, GLA_KW), F32)],
        scratch_shapes=[pltpu.VMEM((BR, GLA_KW), F32), pltpu.VMEM((tc, BR), F32), pltpu.VMEM((tc, GLA_KW), F32),
                        pltpu.VMEM((tc, GLA_KW), F32), pltpu.VMEM((tc, GLA_KW), F32)],
        compiler_params=_cparams(("arbitrary", "arbitrary")),
    )(pgla, pgla, pgla, pgla, pgla, gn)


NEG = -1e30


def _fox_kernel(qi_tab, kj_tab, q_ref, k_ref, v_ref, o_ref, m_ref, acc_ref):
    p = pl.program_id(1)
    qi = qi_tab[p]
    kj = kj_tab[p]
    tq = q_ref.shape[1]
    ntile = tq // 128

    @pl.when(kj == 0)
    def _():
        m_ref[...] = jnp.full_like(m_ref, NEG)
        acc_ref[...] = jnp.zeros_like(acc_ref)

    def step(diag):
        if diag:
            causal = _iota((tq, tq), 1) <= _iota((tq, tq), 0)
        for h in range(HEADS):
            s = _dot_nt(q_ref[h], k_ref[h])
            if diag:
                s = jnp.where(causal, s, NEG)
            tiles = [s[:, 128 * c:128 * (c + 1)] for c in range(ntile)]
            m4 = functools.reduce(jnp.maximum, tiles)
            m_prev = m_ref[h]
            m_new = jnp.maximum(m_prev, jnp.max(m4, axis=1, keepdims=True))
            alpha = jnp.exp2(m_prev - m_new)
            pe = jnp.concatenate([jnp.exp2(t - m_new).astype(BF16) for t in tiles], axis=1)
            acc_ref[h] = alpha * acc_ref[h] + _dot(pe, v_ref[h])
            m_ref[h] = m_new

    @pl.when(kj != qi)
    def _():
        step(False)

    @pl.when(kj == qi)
    def _():
        step(True)
        low = _iota((tq, 128), 1) < FOX_HD
        for pair in range(HEADS // 2):
            a0 = acc_ref[2 * pair]
            a1 = acc_ref[2 * pair + 1]
            oe = jnp.where(low, a0 / a0[:, FOX_HD:FOX_HD + 1], 0.0)
            oo = jnp.where(low, a1 / a1[:, FOX_HD:FOX_HD + 1], 0.0)
            o_ref[:, 128 * pair:128 * pair + 128] = (oe + pltpu.roll(oo, FOX_HD, 1)).astype(BF16)


def _fox_prompt(fq, fk, fv, *, batch, seq_len, tq):
    t = fq.shape[1]
    nq = seq_len // tq
    pairs = [(i, j) for i in range(nq) for j in range(i + 1)]
    qi_tab = jnp.asarray([p[0] for p in pairs], jnp.int32)
    kj_tab = jnp.asarray([p[1] for p in pairs], jnp.int32)
    grid_spec = pltpu.PrefetchScalarGridSpec(
        num_scalar_prefetch=2, grid=(batch, len(pairs)),
        in_specs=[pl.BlockSpec((HEADS, tq, 128), lambda b, p, qt, kt: (0, b * nq + qt[p], 0)),
                  pl.BlockSpec((HEADS, tq, 128), lambda b, p, qt, kt: (0, b * nq + kt[p], 0)),
                  pl.BlockSpec((HEADS, tq, 128), lambda b, p, qt, kt: (0, b * nq + kt[p], 0))],
        out_specs=pl.BlockSpec((tq, BR), lambda b, p, qt, kt: (b * nq + qt[p], 0)),
        scratch_shapes=[pltpu.VMEM((HEADS, tq, 128), F32)] * 2)
    return pl.pallas_call(
        _fox_kernel, grid_spec=grid_spec, out_shape=jax.ShapeDtypeStruct((t, BR), BF16), name="fox_prompt",
        compiler_params=_cparams(("arbitrary", "arbitrary")),
    )(qi_tab, kj_tab, fq, fk, fv)


def _merge_kernel(x_ref, o0_ref, o1_ref, o2_ref, o3_ref, g_ref, wg_ref, bg_ref, wb_ref, wo_ref, out_ref):
    x = x_ref[...]
    h = _rms(x, g_ref[...]).astype(BF16)
    merged = None
    for n, o_ref in enumerate((o0_ref, o1_ref, o2_ref, o3_ref)):
        cols = slice(n * D_MODEL, (n + 1) * D_MODEL)
        gate = _sigmoid(_dot(h, wg_ref[:, cols]) + bg_ref[:, cols])
        term = gate * _dot(o_ref[...], wb_ref[n])
        merged = term if merged is None else merged + term
    out_ref[...] = x + _dot(merged.astype(BF16), wo_ref[...])


def _merge(x2d, branches, g, wg, bg, wb, wo, layer, *, tm):
    t = x2d.shape[0]
    row = lambda c: pl.BlockSpec((tm, c), lambda i: (i, 0))
    return pl.pallas_call(
        _merge_kernel, grid=(t // tm,), name="merge",
        in_specs=[row(D_MODEL), row(BR), row(BR), row(BR), row(BR), _full((1, D_MODEL)),
                  _layer_block((D_MODEL, N_BRANCH * D_MODEL), layer), _full((1, N_BRANCH * D_MODEL)),
                  _layer_block((N_BRANCH, BR, D_MODEL), layer), _layer_block((D_MODEL, D_MODEL), layer)],
        out_specs=row(D_MODEL), out_shape=jax.ShapeDtypeStruct((t, D_MODEL), F32),
        compiler_params=_cparams(("arbitrary",)),
    )(x2d, *branches, g, wg, bg, wb, wo)


def _ffn_kernel(*refs, prompt, tiles_per_batch, final):
    if prompt:
        x_ref, g_ref, wu_ref, cw_ref, cb_ref, wd_ref, gf_ref, out_ref, tail_ref, prev_ref = refs
    else:
        x_ref, g_ref, wu_ref, cw_ref, cb_ref, wd_ref, gf_ref, c0_ref, c1_ref, out_ref, a_ref = refs
    x = x_ref[...]
    tm = x.shape[0]
    h = _rms(x, g_ref[...]).astype(BF16)
    if prompt:
        i = pl.program_id(0)

        @pl.when(i % tiles_per_batch == 0)
        def _():
            prev_ref[...] = jnp.zeros_like(prev_ref)

        rowid = _iota((tm, 1), 0)
    y = x
    for ch in range(D_FF // FF_CHUNK):
        cols = slice(ch * FF_CHUNK, (ch + 1) * FF_CHUNK)
        gcols = slice(D_FF + ch * FF_CHUNK, D_FF + (ch + 1) * FF_CHUNK)
        a = _dot(h, wu_ref[:, cols])
        gate = _dot(h, wu_ref[:, gcols])
        if prompt:
            p1 = prev_ref[SUBLANES - 1:SUBLANES, cols]
            p2 = prev_ref[SUBLANES - 2:SUBLANES - 1, cols]
            a1 = jnp.where(rowid == 0, p1, pltpu.roll(a, 1, 0))
            a2 = jnp.where(rowid == 0, p2, jnp.where(rowid == 1, p1, pltpu.roll(a, 2, 0)))
            prev_ref[:, cols] = a[tm - SUBLANES:tm, :]
            tail_ref[0, :, cols] = a[tm - SUBLANES:tm, :]
        else:
            a1 = c1_ref[:, cols]
            a2 = c0_ref[:, cols]
            a_ref[:, cols] = a
        conv = cb_ref[:, cols] + cw_ref[0:1, cols] * a2 + cw_ref[1:2, cols] * a1 + cw_ref[2:3, cols] * a
        act = (_gelu_tanh(conv) * gate).astype(BF16)
        y = y + _dot(act, wd_ref[cols, :])
    out_ref[...] = _rms(y, gf_ref[...]) if final else y


def _ffn(x2d, g, wu, cw, cb, wd, gfinal, layer, conv0=None, *, prompt, batch, seq_len, tm):
    t = x2d.shape[0]
    tpb = max(seq_len // tm, 1)
    final = layer == wu.shape[0] - 1
    row = lambda c: pl.BlockSpec((tm, c), lambda i: (i, 0))
    in_specs = [row(D_MODEL), _full((1, D_MODEL)), _layer_block((D_MODEL, 2 * D_FF), layer),
                _layer_block((CONV_W, D_FF), layer), _full((1, D_FF)), _layer_block((D_FF, D_MODEL), layer),
                _full((1, D_MODEL))]
    args = [x2d, g, wu, cw, cb, wd, gfinal]
    if prompt:
        out_shape = [jax.ShapeDtypeStruct((t, D_MODEL), F32), jax.ShapeDtypeStruct((batch, SUBLANES, D_FF), F32)]
        out_specs = [row(D_MODEL), pl.BlockSpec((1, SUBLANES, D_FF), lambda i: (i // tpb, 0, 0))]
        scratch = [pltpu.VMEM((SUBLANES, D_FF), F32)]
    else:
        in_specs += [row(D_FF), row(D_FF)]
        args += [conv0[:, 0, :], conv0[:, 1, :]]
        out_shape = [jax.ShapeDtypeStruct((t, D_MODEL), F32), jax.ShapeDtypeStruct((t, D_FF), F32)]
        out_specs = [row(D_MODEL), row(D_FF)]
        scratch = []
    return pl.pallas_call(
        functools.partial(_ffn_kernel, prompt=prompt, tiles_per_batch=tpb, final=final), name="conv_ffn",
        grid=(t // tm,), in_specs=in_specs, out_specs=out_specs, out_shape=out_shape,
        scratch_shapes=scratch, compiler_params=_cparams(("arbitrary",)),
    )(*args)


def _sample_mix_kernel(rq_ref, rk_ref, rv_ref, rg_ref, rs_ref, rgn_ref,
                       gq_ref, gk_ref, gla_ref, gv_ref, gg_ref, gs_ref, ggn_ref,
                       u_ref, sre_ref, sim_ref, abre_ref, abim_ref, bb_ref, cc_ref, d_ref, wg_ref, bg_ref,
                       oret_ref, rsn_ref, ogla_ref, gsn_ref, ossm_ref, sren_ref, simn_ref):
    lg = _ret_log_gamma()
    v = rv_ref[...]
    for h in range(HEADS):
        sn = math.exp(lg[h]) * rs_ref[:, h] + rk_ref[:, h] * v[:, h]
        rsn_ref[:, h] = sn
        o = jnp.sum(rq_ref[:, h] * sn, axis=1)
        oc = o - jnp.mean(o, axis=-1, keepdims=True)
        on = oc * lax.rsqrt(jnp.mean(oc * oc, axis=-1, keepdims=True) + EPS)
        gt = rg_ref[:, h]
        oret_ref[:, h] = on * rgn_ref[h] * (gt * _sigmoid(gt))
    gv = gv_ref[...]
    for h in range(HEADS):
        sn = jnp.exp(gla_ref[:, h]) * gs_ref[:, h] + gk_ref[:, h] * gv[:, h]
        gsn_ref[:, h] = sn
        o = jnp.sum(gq_ref[:, h] * sn, axis=1)
        on = o * lax.rsqrt(jnp.mean(o * o, axis=-1, keepdims=True) + EPS)
        gt = gg_ref[:, h]
        ogla_ref[:, h] = on * ggn_ref[h] * (gt * _sigmoid(gt))
    u = u_ref[...]
    bu = _dot(u.astype(BF16), bb_ref[...])
    dr, di = _cmul(abre_ref[...], abim_ref[...], sre_ref[...], sim_ref[...])
    xr = bu[:, :SSM_N] + dr
    xi = bu[:, SSM_N:] + di
    sren_ref[...] = xr
    simn_ref[...] = xi
    xcat = jnp.concatenate([xr.astype(BF16), xi.astype(BF16)], axis=1)
    y = _dot(xcat, cc_ref[...]) + d_ref[...] * u
    yg = _gelu_tanh(y)
    ossm_ref[...] = yg * _sigmoid(_dot(yg.astype(BF16), wg_ref[...]) + bg_ref[...])


def _sample_mix(pret, pgla, su, st_ret, st_gla, st_re, st_im, rgn, ggn, ab_re, ab_im, bbcat, cccat, d_skip, w_glu, b_glu):
    n = pret.shape[0]

    def unpack_qk(a):
        return jnp.transpose(a.reshape(n, 2, HEADS, 32), (0, 2, 1, 3)).reshape(n, HEADS, RET_DK, 1)

    rq = unpack_qk(pret[:, 0:256])
    rk = unpack_qk(pret[:, 256:512])
    rv = pret[:, 512:768].reshape(n, HEADS, 1, 64)
    rg = pret[:, 768:1024].reshape(n, HEADS, 64)
    gq = pgla[:, G_Q:G_K].reshape(n, HEADS, GLA_DK, 1)
    gk = pgla[:, G_K:G_LA].reshape(n, HEADS, GLA_DK, 1)
    gla = pgla[:, G_LA:G_LA + 128].reshape(n, HEADS, GLA_DK, 1)
    gv = pgla[:, G_V:G_G].reshape(n, HEADS, 1, 64)
    gg = pgla[:, G_G:G_Q].reshape(n, HEADS, 64)
    out_shape = [jax.ShapeDtypeStruct((n, HEADS, 64), F32), jax.ShapeDtypeStruct(st_ret.shape, F32),
                 jax.ShapeDtypeStruct((n, HEADS, 64), F32), jax.ShapeDtypeStruct(st_gla.shape, F32),
                 jax.ShapeDtypeStruct((n, BR), F32), jax.ShapeDtypeStruct((n, SSM_N), F32),
                 jax.ShapeDtypeStruct((n, SSM_N), F32)]
    outs = pl.pallas_call(
        _sample_mix_kernel, out_shape=out_shape, name="sample_mixers",
        compiler_params=pltpu.CompilerParams(vmem_limit_bytes=VMEM_LIMIT),
    )(rq, rk, rv, rg, st_ret, rgn.reshape(HEADS, 1, 64), gq, gk, gla, gv, gg, st_gla, ggn.reshape(HEADS, 1, 64),
      su, st_re.reshape(n, SSM_N), st_im.reshape(n, SSM_N), ab_re, ab_im, bbcat, cccat, d_skip, w_glu, b_glu)
    o_ret, ret_new, o_gla, gla_new, o_ssm, re_new, im_new = outs
    return (o_ret.reshape(n, BR), o_ssm, o_gla.reshape(n, BR), ret_new, gla_new,
            re_new.reshape(n, SSM_G, SSM_P), im_new.reshape(n, SSM_G, SSM_P))


def _fox_sample_kernel(pt_ref, q_ref, kn_ref, vn_ref, lfn_ref, lfpool_ref, ck_hbm, cv_hbm, o_ref,
                       kbuf, vbuf, sem, qb_ref, m_ref, l_ref, acc_ref, car_ref, *, npp, n_pages, n_samples, layer):
    steps = n_pages // npp
    total = n_samples * steps

    def page_id(g, i):
        b = lax.div(g, steps)
        p = lax.rem(g, steps)
        return pt_ref[b * n_pages + n_pages - 1 - (p * npp + i)]

    def page_copies(g, slot):
        cps = []
        for i in range(npp):
            pid = page_id(g, i)
            cps.append(pltpu.make_async_copy(ck_hbm.at[layer, pid], kbuf.at[slot, i], sem.at[0, slot]))
            cps.append(pltpu.make_async_copy(cv_hbm.at[layer, pid], vbuf.at[slot, i], sem.at[1, slot]))
        return cps

    def head_rows(x):
        out = jnp.zeros((HEADS, x.shape[1]), F32)
        for h in range(HEADS):
            r = jnp.sum(x[FOX_HD * h:FOX_HD * (h + 1), :], axis=0, keepdims=True)
            out = jnp.where(_iota(out.shape, 0) == h, r, out)
        return out

    for g0 in range(min(PAGE_SLOTS, total)):
        for cp in page_copies(g0, g0):
            cp.start()

    def body(g, carry):
        slot = lax.rem(g, PAGE_SLOTS)
        b = lax.div(g, steps)
        p = lax.rem(g, steps)
        for cp in page_copies(g, slot):
            cp.wait()

        @pl.when(p == 0)
        def _():
            qcol = q_ref[b]
            qb_ref[...] = jnp.broadcast_to(qcol, qb_ref.shape)
            m_ref[...] = jnp.broadcast_to(head_rows(qcol * kn_ref[b]), m_ref.shape)
            l_ref[...] = jnp.ones_like(l_ref)
            acc_ref[...] = jnp.where(_iota(acc_ref.shape, 1) == 0, vn_ref[b], 0.0)
            car_ref[...] = jnp.broadcast_to(lfn_ref[b], car_ref.shape)

        qb = qb_ref[...]
        car = car_ref[...]
        logits = []
        lf_all = jnp.concatenate([lfpool_ref[page_id(g, i)] for i in range(npp)], axis=0)
        upper = (_iota((PAGE, PAGE), 0) >= _iota((PAGE, PAGE), 1)).astype(BF16)
        suf_all = _dot3_right(lf_all, upper)
        for i in range(npp):
            lf = lf_all[SUBLANES * i:SUBLANES * i + HEADS, :]
            suf = suf_all[SUBLANES * i:SUBLANES * i + HEADS, :]
            s = head_rows(kbuf[slot, i] * qb) + ((suf - lf) + car) * LOG2E
            car = car + suf[:, 0:1]
            logits.append(s)
        m_prev = m_ref[...]
        m_new = jnp.maximum(m_prev, jnp.max(functools.reduce(jnp.maximum, logits), axis=1, keepdims=True))
        alpha = jnp.exp2(m_prev - m_new)
        probs = [jnp.exp2(s - m_new) for s in logits]
        l_ref[...] = alpha * l_ref[...] + jnp.sum(functools.reduce(jnp.add, probs), axis=1, keepdims=True)
        m_ref[...] = m_new
        car_ref[...] = car
        for h in range(HEADS):
            rows = slice(FOX_HD * h, FOX_HD * (h + 1))
            a = alpha[h:h + 1, :] * acc_ref[rows, :]
            for i in range(npp):
                a = a + vbuf[slot, i, rows, :] * probs[i][h:h + 1, :]
            acc_ref[rows, :] = a

        @pl.when(p == steps - 1)
        def _():
            for h in range(HEADS):
                rows = slice(FOX_HD * h, FOX_HD * (h + 1))
                o_ref[b, rows, :] = jnp.sum(acc_ref[rows, :], axis=1, keepdims=True) / l_ref[h:h + 1, 0:1]

        @pl.when(g + PAGE_SLOTS < total)
        def _():
            for cp in page_copies(g + PAGE_SLOTS, slot):
                cp.start()

        return carry

    lax.fori_loop(0, total, body, 0)


def _fox_sample(page_table, fq, kn, vn, lfn, cache_kt, cache_vt, cache_lft, layer):
    n, n_pages = page_table.shape
    npp = PAGES_PER_STEP
    while n_pages % npp:
        npp //= 2
    pt = page_table.reshape(-1)
    n_pool = cache_lft.shape[1]
    col3 = lambda r: pl.BlockSpec((n, r, 1), lambda i, pt_: (0, 0, 0))
    in_specs = [col3(BR), col3(BR), col3(BR), col3(HEADS),
                pl.BlockSpec((None, n_pool, SUBLANES, PAGE), lambda i, pt_: (layer, 0, 0, 0)),
                pl.BlockSpec(memory_space=pl.ANY), pl.BlockSpec(memory_space=pl.ANY)]
    grid_spec = pltpu.PrefetchScalarGridSpec(
        num_scalar_prefetch=1, grid=(1,), in_specs=in_specs, out_specs=col3(BR),
        scratch_shapes=[pltpu.VMEM((PAGE_SLOTS, npp, BR, PAGE), F32), pltpu.VMEM((PAGE_SLOTS, npp, BR, PAGE), F32),
                        pltpu.SemaphoreType.DMA((2, PAGE_SLOTS)),
                        pltpu.VMEM((BR, PAGE), F32), pltpu.VMEM((HEADS, PAGE), F32), pltpu.VMEM((HEADS, PAGE), F32),
                        pltpu.VMEM((BR, PAGE), F32), pltpu.VMEM((HEADS, PAGE), F32)])
    out = pl.pallas_call(
        functools.partial(_fox_sample_kernel, npp=npp, n_pages=n_pages, n_samples=n, layer=layer),
        grid_spec=grid_spec, out_shape=jax.ShapeDtypeStruct((n, BR, 1), F32), name="fox_sample",
        compiler_params=_cparams(("arbitrary",)),
    )(pt, fq[:, :, None], kn[:, :, None], vn[:, :, None], lfn[:, :, None], cache_lft, cache_kt, cache_vt)
    return out.reshape(n, BR)


def _pack_w_in(w):
    def rope_pack(blk):
        d = blk.shape[0]
        b4 = blk.reshape(d, D_MODEL, HEADS, 2, 32)
        return jnp.transpose(b4, (0, 1, 3, 2, 4)).reshape(d, D_MODEL, 256)

    depth = w.shape[0]
    small = jnp.zeros((depth, D_MODEL, 128), w.dtype)
    small = small.at[:, :, 0:HEADS].set(w[:, :, N_FF:N_FF + HEADS])
    small = small.at[:, :, HEADS:HEADS + GLA_RANK].set(w[:, :, N_GA:N_GA + GLA_RANK])
    parts = [rope_pack(w[:, :, N_RQ:N_RK]), rope_pack(w[:, :, N_RK:N_RV]) * (RET_DK ** -0.5),
             w[:, :, N_RV:N_GQ],
             w[:, :, N_GQ:N_GA],
             w[:, :, N_FQ:N_FK] * FOX_QSCALE, w[:, :, N_FK:N_FF], small]
    return jnp.concatenate(parts, axis=-1).astype(BF16)


def _rope_tables(pos):
    inv = ROPE_BASE ** (-jnp.arange(32, dtype=F32) * 2.0 / RET_DK)
    ang = pos.astype(F32)[:, None] * inv[None, :]
    return jnp.tile(jnp.cos(ang), (1, HEADS)), jnp.tile(jnp.sin(ang), (1, HEADS))


def _fox_consts():
    eall = np.zeros((128, 8 * 128), np.float32)
    for h in range(HEADS):
        for part in range(3):
            eall[part * HEADS + h, 128 * h + FOX_HD + part] = 1.0
            eall[part * HEADS + h, 128 * (HEADS + h) + FOX_HD + 3 + part] = -1.0
    cq = np.zeros((1, 128), np.float32)
    ck = np.zeros((1, 128), np.float32)
    cq[0, FOX_HD + 3:FOX_HD + 6] = 1.0
    ck[0, FOX_HD:FOX_HD + 3] = 1.0
    return jnp.asarray(eall, BF16), jnp.asarray(cq), jnp.asarray(ck)


def _diag_blocks(s, nh, dk, dv):
    b = s.shape[0]
    s5 = s.reshape(b, nh, dk, nh, dv)
    idx = jnp.arange(nh)
    return jnp.transpose(s5[:, idx, :, idx, :], (1, 0, 2, 3))


def kernel(x_prompt, x_sample, cache_k, cache_v, cache_logf, state_ret, state_ssm_re, state_ssm_im, state_gla, state_conv, page_table, norm1_g, w_in, w_mgate, b_mgate, ret_norm_g, ssm_a_re, ssm_a_im, ssm_log_dt, ssm_b_re, ssm_b_im, ssm_c_re, ssm_c_im, ssm_d, ssm_w_glu, ssm_b_glu, gla_w_a2, gla_b_a, gla_norm_g, fox_b_f, w_branch, w_out, norm2_g, ffn_w_up, ffn_conv_w, ffn_conv_b, ffn_w_down, final_norm_g):
    depth = w_in.shape[0]
    B, L, _ = x_prompt.shape
    NS = x_sample.shape[0]
    n_pages = page_table.shape[1]
    past = n_pages * PAGE
    T = B * L
    tm = min(512, L)
    n_pool = cache_k.shape[1]

    w_in_p = _pack_w_in(w_in)
    wa2 = jnp.zeros((depth, 128, 128), F32).at[:, HEADS:HEADS + GLA_RANK, :].set(gla_w_a2).astype(BF16)
    bfrow = jnp.zeros((depth, 1, 128), F32).at[:, 0, 0:HEADS].set(fox_b_f)
    wg_b = w_mgate.astype(BF16)
    wb_b = w_branch.astype(BF16)
    wo_b = w_out.astype(BF16)
    wu_b = ffn_w_up.astype(BF16)
    wd_b = ffn_w_down.astype(BF16)
    wglu_b = ssm_w_glu.astype(BF16)
    ab_re, ab_im, bbcat = _s5_prep(ssm_a_re, ssm_a_im, ssm_log_dt, ssm_b_re, ssm_b_im)
    cccat = _s5_cmat(ssm_c_re, ssm_c_im)
    cos_p, sin_p = _rope_tables(jnp.arange(L))
    cos_s, sin_s = _rope_tables(jnp.full((NS,), past))
    fox_consts = _fox_consts()
    ckt = jnp.transpose(cache_k, (0, 1, 3, 4, 2)).reshape(depth, n_pool, BR, PAGE)
    cvt = jnp.transpose(cache_v, (0, 1, 3, 4, 2)).reshape(depth, n_pool, BR, PAGE)
    clft = jnp.pad(jnp.transpose(cache_logf, (0, 1, 3, 2)), ((0, 0), (0, 0), (0, SUBLANES - HEADS), (0, 0)))

    row = lambda a, l: a[l][None, :]
    xp = x_prompt.reshape(T, D_MODEL)
    xs = x_sample.reshape(NS, D_MODEL)
    outs_p, outs_s = [], []
    stacked = (jnp.zeros((depth, B, BR, L), F32), jnp.zeros((depth, B, BR, L), F32),
               jnp.zeros((depth, B, HEADS, L), F32))
    for l in range(depth):
        (pret, su, pgla, kt_all, vt_all, lft_all, fq, fk, fv) = _inproj(
            xp, row(norm1_g, l), w_in_p, wa2, row(gla_b_a, l), bfrow[l], cos_p, sin_p, fox_consts, l, stacked,
            prompt=True, seq_len=L, tm=tm)
        stacked = (kt_all, vt_all, lft_all)
        o_ret, s_ret = _retention(pret, row(ret_norm_g, l), batch=B, seq_len=L, tc=tm)
        o_ssm, s_re, s_im = _s5(su, ab_re[l], ab_im[l], bbcat, cccat, row(ssm_d, l), wglu_b,
                                row(ssm_b_glu, l), l, batch=B, seq_len=L, tm=tm)
        o_gla, s_gla = _gla(pgla, row(gla_norm_g, l), batch=B, seq_len=L, tc=tm)
        o_fox = _fox_prompt(fq, fk, fv, batch=B, seq_len=L, tq=min(FOX_TQ, L))
        xp = _merge(xp, (o_ret, o_ssm, o_gla, o_fox), row(norm1_g, l), wg_b, row(b_mgate, l), wb_b, wo_b, l, tm=tm)
        xp, tail = _ffn(xp, row(norm2_g, l), wu_b, ffn_conv_w, row(ffn_conv_b, l), wd_b, final_norm_g[None, :], l,
                        prompt=True, batch=B, seq_len=L, tm=tm)
        sr = jnp.transpose(s_ret.reshape(B, 2, HEADS, 32, BR), (0, 2, 1, 3, 4)).reshape(B, BR, BR)
        outs_p.append((_diag_blocks(sr, HEADS, RET_DK, 64), s_re.reshape(B, SSM_G, SSM_P), s_im.reshape(B, SSM_G, SSM_P),
                       _diag_blocks(jnp.transpose(s_gla, (0, 2, 1)), HEADS, GLA_DK, 64),
                       tail[:, SUBLANES - (CONV_W - 1):, :]))
        (pret, su, pgla, knat, vnat, lf, fqn) = _inproj(
            xs, row(norm1_g, l), w_in_p, wa2, row(gla_b_a, l), bfrow[l], cos_s, sin_s, None, l,
            prompt=False, seq_len=NS, tm=NS)
        (o_ret, o_ssm, o_gla, ret_new, gla_new, re_new, im_new) = _sample_mix(
            pret, pgla, su, state_ret[l], state_gla[l], state_ssm_re[l], state_ssm_im[l],
            ret_norm_g[l], gla_norm_g[l], ab_re[l], ab_im[l], bbcat[l], cccat[l], row(ssm_d, l), wglu_b[l],
            row(ssm_b_glu, l))
        lfs = lf[:, :HEADS]
        o_fox = _fox_sample(page_table, fqn, knat, vnat, lfs, ckt, cvt, clft, l)
        xs = _merge(xs, (o_ret.astype(BF16), o_ssm.astype(BF16), o_gla.astype(BF16), o_fox.astype(BF16)),
                    row(norm1_g, l), wg_b, row(b_mgate, l), wb_b, wo_b, l, tm=NS)
        xs, a_s = _ffn(xs, row(norm2_g, l), wu_b, ffn_conv_w, row(ffn_conv_b, l), wd_b, final_norm_g[None, :], l,
                       state_conv[l], prompt=False, batch=NS, seq_len=NS, tm=NS)
        new_conv = jnp.stack([state_conv[l][:, 1, :], a_s], axis=1)
        outs_s.append((ret_new, re_new, im_new, gla_new, new_conv,
                       knat.reshape(NS, 1, HEADS, FOX_HD), vnat.reshape(NS, 1, HEADS, FOX_HD), lfs.reshape(NS, 1, HEADS)))

    y_prompt = xp.reshape(B, L, D_MODEL)
    y_sample = xs.reshape(NS, 1, D_MODEL)

    def stk(outs, i):
        return jnp.stack([o[i] for o in outs], axis=0)

    kt_all, vt_all, lft_all = stacked
    k_prompt = jnp.transpose(kt_all.reshape(depth, B, HEADS, FOX_HD, L), (0, 1, 4, 2, 3))
    v_prompt = jnp.transpose(vt_all.reshape(depth, B, HEADS, FOX_HD, L), (0, 1, 4, 2, 3))
    logf_prompt = jnp.transpose(lft_all, (0, 1, 3, 2))
    return (y_prompt, y_sample,
            k_prompt, v_prompt, logf_prompt,
            stk(outs_s, 5), stk(outs_s, 6), stk(outs_s, 7),
            stk(outs_p, 0), stk(outs_s, 0),
            stk(outs_p, 1), stk(outs_p, 2), stk(outs_s, 1), stk(outs_s, 2),
            stk(outs_p, 3), stk(outs_s, 3),
            stk(outs_p, 4), stk(outs_s, 4))
```
